```python
import math
import jax
import jax.numpy as jnp
from jax import lax
import numpy as np

D_MODEL = 1024
BATCH = 4
SEQ = 8192
DEPTH = 2

GRID_W = 64
CTX_LEN = 256
SSM_WIDTH = D_MODEL // 4
SSM_GROUP = 16
SSM_GROUPS = SSM_WIDTH // SSM_GROUP
SSM_STATE = 64
SSM_DT_MIN = 1e-3
SSM_DT_MAX = 1e-1
SSM_RE_MAX = -1e-4
DIFF_HEADS = D_MODEL // 256
DIFF_DH = 64
DIFF_WIDTH = DIFF_HEADS * 2 * DIFF_DH
NA_HEADS = D_MODEL // 256
NA_DH = 64
NA_WIDTH = NA_HEADS * NA_DH
NA_WIN_H = 8
NA_WIN_W = 16
ATTN_BLOCK = 128
ROPE_THETA = 10000.0
ROPE_FREQS = DIFF_DH // 4
IN_SPLITS = (SSM_WIDTH, DIFF_WIDTH, DIFF_WIDTH, DIFF_WIDTH, NA_WIDTH, NA_WIDTH, NA_WIDTH, D_MODEL, D_MODEL, D_MODEL)
IN_WIDTH = sum(IN_SPLITS)
PEER_HEADS = 8
PEER_NKEYS = 128
PEER_EXPERTS = PEER_NKEYS ** 2
PEER_DHALF = 128
PEER_TOPK = 16
PEER_CHUNK = 128
RMS_EPS = 1e-6
NEG_INF = -1e30

kernel_name = 'hybrid_s5_diffattn_natten_peer'


def rmsnorm(x, g):
    xf = x.astype(jnp.float32)
    y = xf * lax.rsqrt(jnp.mean(xf * xf, axis=-1, keepdims=True) + RMS_EPS)
    return (y * g.astype(jnp.float32)).astype(x.dtype)


def rope_half(x, ang):
    f = ang.shape[-1]
    cos = jnp.cos(ang).astype(x.dtype)
    sin = jnp.sin(ang).astype(x.dtype)
    x1, x2 = x[..., :f], x[..., f:]
    return jnp.concatenate([x1 * cos - x2 * sin, x2 * cos + x1 * sin], axis=-1)


def rope_2d(x, ang_r, ang_c):
    half = x.shape[-1] // 2
    return jnp.concatenate([rope_half(x[..., :half], ang_r), rope_half(x[..., half:], ang_c)], axis=-1)


def _lin_combine(e1, e2):
    a1, b1 = e1
    a2, b2 = e2
    return a1 * a2, a2 * b1 + b2


def ssm_discretise(lam_re, lam_im, log_step, b_re, b_im):
    lam = lax.complex(jnp.minimum(lam_re.astype(jnp.float32), SSM_RE_MAX), lam_im.astype(jnp.float32))
    lam_dt = lam * jnp.exp(log_step.astype(jnp.float32))[:, None]
    lam_bar = jnp.exp(lam_dt)
    b = lax.complex(b_re.astype(jnp.float32), b_im.astype(jnp.float32))
    b_bar = ((lam_bar - 1.0) / lam)[..., None] * b
    return lam_dt, lam_bar, b_bar


def ssm_scan(u, lam_bar, b_bar, reverse):
    bu = jnp.einsum('blgp,gnp->blgn', u.astype(jnp.complex64), b_bar)
    a = jnp.broadcast_to(lam_bar, (1, u.shape[1]) + lam_bar.shape)
    _, s = lax.associative_scan(_lin_combine, (a, bu), reverse=reverse, axis=1)
    return s


def ssm_glu(y, glu_w, dtype):
    bn, length = y.shape[:2]
    z = jax.nn.gelu(y.reshape(bn, length, SSM_WIDTH)).astype(dtype) @ glu_w
    val, gate = jnp.split(z, 2, axis=-1)
    return val * jax.nn.sigmoid(gate)


def ssm_mixer(u_lat, u_ctx, lam_re, lam_im, log_step, b_re, b_im, c_re, c_im, d, glu_w, with_ctx):
    bn, length, _ = u_lat.shape
    ul = u_lat.astype(jnp.float32).reshape(bn, length, SSM_GROUPS, SSM_GROUP)
    uc = u_ctx.astype(jnp.float32).reshape(bn, u_ctx.shape[1], SSM_GROUPS, SSM_GROUP)
    dd = d.astype(jnp.float32).reshape(SSM_GROUPS, SSM_GROUP)
    y_lat = dd * ul
    y_ctx = dd * uc if with_ctx else None
    for direction, reverse in ((0, False), (1, True)):
        lam_dt, lam_bar, b_bar = ssm_discretise(lam_re[direction], lam_im[direction], log_step[direction],
                                                b_re[direction], b_im[direction])
        cm = lax.complex(c_re[direction].astype(jnp.float32), c_im[direction].astype(jnp.float32))
        s_ctx = ssm_scan(uc, lam_bar, b_bar, reverse)
        s0 = s_ctx[:, 0] if reverse else s_ctx[:, -1]
        steps = (jnp.arange(length, 0, -1) if reverse else jnp.arange(1, length + 1)).astype(jnp.float32)
        carry = jnp.exp(lam_dt[None] * steps[:, None, None])
        s_lat = ssm_scan(ul, lam_bar, b_bar, reverse) + carry[None] * s0[:, None]
        y_lat = y_lat + jnp.einsum('blgn,gpn->blgp', s_lat, cm).real
        if with_ctx:
            y_ctx = y_ctx + jnp.einsum('blgn,gpn->blgp', s_ctx, cm).real
    out_lat = ssm_glu(y_lat, glu_w, u_lat.dtype)
    out_ctx = ssm_glu(y_ctx, glu_w, u_lat.dtype) if with_ctx else None
    return out_lat, out_ctx


def diff_attend(q, k, v, lam):
    bn, nh, _, lq, dh = q.shape
    nb = lq // ATTN_BLOCK
    qb = jnp.moveaxis(q.reshape(bn, nh, 2, nb, ATTN_BLOCK, dh), 3, 0)
    scale = dh ** -0.5

    def block(qi):
        s = jnp.einsum('bhmqd,bhmkd->bhmqk', qi, k).astype(jnp.float32) * scale
        p = jax.nn.softmax(s, axis=-1)
        a = (p[:, :, 0] - lam * p[:, :, 1]).astype(v.dtype)
        return jnp.einsum('bhqk,bhkd->bhqd', a, v)

    o = lax.map(block, qb)
    return jnp.moveaxis(o, 0, 2).reshape(bn, nh, lq, v.shape[-1])


def diff_mixer(q_lat, k_lat, v_lat, q_ctx, k_ctx, v_ctx, lam_qk, subln_g, lam_init, ang_r, ang_c, with_ctx):
    def qk_heads(t):
        return t.reshape(t.shape[0], t.shape[1], DIFF_HEADS, 2, DIFF_DH).transpose(0, 2, 3, 1, 4)

    def v_heads(t):
        return t.reshape(t.shape[0], t.shape[1], DIFF_HEADS, 2 * DIFF_DH).transpose(0, 2, 1, 3)

    def finish(o):
        o = rmsnorm(o, subln_g) * (1.0 - lam_init)
        return o.transpose(0, 2, 1, 3).reshape(o.shape[0], o.shape[2], DIFF_WIDTH)

    lq = lam_qk.astype(jnp.float32)
    lam = jnp.exp(jnp.sum(lq[0] * lq[1])) - jnp.exp(jnp.sum(lq[2] * lq[3])) + lam_init
    ql = rope_2d(qk_heads(q_lat), ang_r, ang_c)
    kl = rope_2d(qk_heads(k_lat), ang_r, ang_c)
    kc = qk_heads(k_ctx)
    vc = v_heads(v_ctx)
    k_all = jnp.concatenate([kc, kl], axis=3)
    v_all = jnp.concatenate([vc, v_heads(v_lat)], axis=2)
    out_lat = finish(diff_attend(ql, k_all, v_all, lam))
    out_ctx = finish(diff_attend(qk_heads(q_ctx), kc, vc, lam)) if with_ctx else None
    return out_lat, out_ctx


def dense_attend(q, k, v):
    s = jnp.einsum('bhqd,bhkd->bhqk', q, k).astype(jnp.float32) * q.shape[-1] ** -0.5
    p = jax.nn.softmax(s, axis=-1).astype(v.dtype)
    return jnp.einsum('bhqk,bhkd->bhqd', p, v)


def na_mixer(q_lat, k_lat, v_lat, q_ctx, k_ctx, v_ctx, rpb, with_ctx):
    bn, length, _ = q_lat.shape
    rows = length // GRID_W
    kh = min(NA_WIN_H, rows)

    def grid_heads(t):
        return t.reshape(bn, rows, GRID_W, NA_HEADS, NA_DH).transpose(0, 3, 1, 2, 4)

    def seq_heads(t):
        return t.reshape(bn, t.shape[1], NA_HEADS, NA_DH).transpose(0, 2, 1, 3)

    qg, kg, vg = grid_heads(q_lat), grid_heads(k_lat), grid_heads(v_lat)
    kc, vc = seq_heads(k_ctx), seq_heads(v_ctx)
    row_idx = np.clip(np.arange(rows) - kh // 2, 0, rows - kh)[:, None] + np.arange(kh)[None, :]
    kw = kg[:, :, row_idx].reshape(bn, NA_HEADS, rows, kh * GRID_W, NA_DH)
    vw = vg[:, :, row_idx].reshape(bn, NA_HEADS, rows, kh * GRID_W, NA_DH)
    cols = np.arange(GRID_W)
    col_start = np.clip(cols - NA_WIN_W // 2, 0, GRID_W - NA_WIN_W)
    col_ok = (cols[None, :] >= col_start[:, None]) & (cols[None, :] < col_start[:, None] + NA_WIN_W)
    mask = np.broadcast_to(col_ok[:, None, :], (GRID_W, kh, GRID_W)).reshape(GRID_W, kh * GRID_W)
    dr_idx = row_idx - np.arange(rows)[:, None] + NA_WIN_H - 1
    dc_idx = np.clip(cols[None, :] - cols[:, None] + NA_WIN_W - 1, 0, 2 * NA_WIN_W - 2)
    bias = rpb[:, dr_idx[:, None, :, None], dc_idx[None, :, None, :]].reshape(NA_HEADS, rows, GRID_W, kh * GRID_W)
    scale = NA_DH ** -0.5
    s_loc = jnp.einsum('bhrqd,bhrkd->bhrqk', qg, kw).astype(jnp.float32) * scale + bias.astype(jnp.float32)
    s_loc = jnp.where(mask, s_loc, NEG_INF)
    s_ctx = jnp.einsum('bhrqd,bhkd->bhrqk', qg, kc).astype(jnp.float32) * scale
    p = jax.nn.softmax(jnp.concatenate([s_loc, s_ctx], axis=-1), axis=-1).astype(vg.dtype)
    nloc = kh * GRID_W
    o = (jnp.einsum('bhrqk,bhrkd->bhrqd', p[..., :nloc], vw)
         + jnp.einsum('bhrqk,bhkd->bhrqd', p[..., nloc:], vc))
    out_lat = o.transpose(0, 2, 3, 1, 4).reshape(bn, length, NA_WIDTH)
    out_ctx = None
    if with_ctx:
        oc = dense_attend(seq_heads(q_ctx), kc, vc)
        out_ctx = oc.transpose(0, 2, 1, 3).reshape(bn, q_ctx.shape[1], NA_WIDTH)
    return out_lat, out_ctx


def merge_branches(y_ssm, y_diff, y_na, g_ssm, g_diff, g_na, w_br_ssm, w_br_diff, w_br_na, w_out):
    m = (jax.nn.sigmoid(g_ssm) * (y_ssm @ w_br_ssm)
         + jax.nn.sigmoid(g_diff) * (y_diff @ w_br_diff)
         + jax.nn.sigmoid(g_na) * (y_na @ w_br_na))
    return m @ w_out


def peer_ffn(h, wq, subkeys, u_tab, v_tab):
    bn, length, dm = h.shape
    nc = length // PEER_CHUNK
    hc = h.reshape(bn, nc, PEER_CHUNK, dm).transpose(1, 0, 2, 3)

    def chunk(hb):
        q = (hb @ wq).reshape(bn, PEER_CHUNK, PEER_HEADS, 2, PEER_DHALF)
        s = jnp.einsum('bchxd,hxnd->bchxn', q, subkeys).astype(jnp.float32)
        sv, si = lax.top_k(s, PEER_TOPK)
        cand = (sv[..., 0, :, None] + sv[..., 1, None, :]).reshape(bn, PEER_CHUNK, PEER_HEADS, PEER_TOPK * PEER_TOPK)
        cidx = (si[..., 0, :, None] * PEER_NKEYS + si[..., 1, None, :]).reshape(bn, PEER_CHUNK, PEER_HEADS, PEER_TOPK * PEER_TOPK)
        best, pos = lax.top_k(cand, PEER_TOPK)
        eidx = jnp.take_along_axis(cidx, pos, axis=-1)
        g = jax.nn.softmax(best, axis=-1).astype(hb.dtype)
        act = jax.nn.gelu(jnp.einsum('bchkd,bcd->bchk', u_tab[eidx], hb))
        return jnp.einsum('bchk,bchkd->bcd', g * act, v_tab[eidx])

    out = lax.map(chunk, hc)
    return out.transpose(1, 0, 2, 3).reshape(bn, length, dm)


def setup_inputs(seed: int = 0) -> dict:
    key = jax.random.key(seed)
    ks = jax.random.split(key, 32)
    f32 = jnp.float32

    def nrm(k, shape, scale):
        return jax.random.normal(k, shape, f32) * scale

    G, N, P = SSM_GROUPS, SSM_STATE, SSM_GROUP
    n_idx = jnp.arange(N, dtype=f32)
    return {
        'x': nrm(ks[0], (BATCH, SEQ, D_MODEL), 1.0),
        'c': nrm(ks[1], (BATCH, D_MODEL), 1.0),
        'ctx': nrm(ks[2], (BATCH, CTX_LEN, D_MODEL), 1.0),
        'c_ctx': nrm(ks[3], (D_MODEL,), 1.0),
        'ada_w': nrm(ks[4], (DEPTH, D_MODEL, 6 * D_MODEL), 0.5 * D_MODEL ** -0.5),
        'ada_b': nrm(ks[5], (DEPTH, 6 * D_MODEL), 0.01),
        'norm1_g': 1.0 + nrm(ks[6], (DEPTH, D_MODEL), 0.01),
        'norm2_g': 1.0 + nrm(ks[7], (DEPTH, D_MODEL), 0.01),
        'w_in': nrm(ks[8], (DEPTH, D_MODEL, IN_WIDTH), D_MODEL ** -0.5),
        'ssm_lambda_re': -0.5 + nrm(ks[9], (DEPTH, 2, G, N), 0.01),
        'ssm_lambda_im': math.pi * n_idx + nrm(ks[10], (DEPTH, 2, G, N), 0.01),
        'ssm_log_step': jax.random.uniform(ks[11], (DEPTH, 2, G), f32, math.log(SSM_DT_MIN), math.log(SSM_DT_MAX)),
        'ssm_b_re': nrm(ks[12], (DEPTH, 2, G, N, P), (2 * P) ** -0.5),
        'ssm_b_im': nrm(ks[13], (DEPTH, 2, G, N, P), (2 * P) ** -0.5),
        'ssm_c_re': nrm(ks[14], (DEPTH, 2, G, P, N), (2 * N) ** -0.5),
        'ssm_c_im': nrm(ks[15], (DEPTH, 2, G, P, N), (2 * N) ** -0.5),
        'ssm_d': nrm(ks[16], (DEPTH, SSM_WIDTH), 1.0),
        'ssm_glu_w': nrm(ks[17], (DEPTH, SSM_WIDTH, 2 * SSM_WIDTH), SSM_WIDTH ** -0.5),
        'diff_lambda': nrm(ks[18], (DEPTH, 4, DIFF_DH), 0.1),
        'diff_subln_g': 1.0 + nrm(ks[19], (DEPTH, 2 * DIFF_DH), 0.01),
        'na_rpb': nrm(ks[20], (DEPTH, NA_HEADS, 2 * NA_WIN_H - 1, 2 * NA_WIN_W - 1), 0.02),
        'w_br_ssm': nrm(ks[21], (DEPTH, SSM_WIDTH, D_MODEL), SSM_WIDTH ** -0.5),
        'w_br_diff': nrm(ks[22], (DEPTH, DIFF_WIDTH, D_MODEL), DIFF_WIDTH ** -0.5),
        'w_br_na': nrm(ks[23], (DEPTH, NA_WIDTH, D_MODEL), NA_WIDTH ** -0.5),
        'w_out': nrm(ks[24], (DEPTH, D_MODEL, D_MODEL), D_MODEL ** -0.5),
        'peer_wq': nrm(ks[25], (DEPTH, D_MODEL, PEER_HEADS * 2 * PEER_DHALF), D_MODEL ** -0.5),
        'peer_subkeys': nrm(ks[26], (DEPTH, PEER_HEADS, 2, PEER_NKEYS, PEER_DHALF), PEER_DHALF ** -0.5),
        'peer_u': nrm(ks[27], (DEPTH, PEER_EXPERTS, D_MODEL), D_MODEL ** -0.5),
        'peer_v': nrm(ks[28], (DEPTH, PEER_EXPERTS, D_MODEL), 0.5),
        'final_norm_g': 1.0 + nrm(ks[29], (D_MODEL,), 0.01),
    }


def reference(x, c, ctx, c_ctx, ada_w, ada_b, norm1_g, norm2_g, w_in, ssm_lambda_re, ssm_lambda_im,
              ssm_log_step, ssm_b_re, ssm_b_im, ssm_c_re, ssm_c_im, ssm_d, ssm_glu_w, diff_lambda,
              diff_subln_g, na_rpb, w_br_ssm, w_br_diff, w_br_na, w_out, peer_wq, peer_subkeys, peer_u,
              peer_v, final_norm_g):
    length = x.shape[1]
    t = jnp.arange(length)
    freqs = ROPE_THETA ** (-jnp.arange(ROPE_FREQS, dtype=jnp.float32) / ROPE_FREQS)
    ang_r = (t // GRID_W).astype(jnp.float32)[:, None] * freqs
    ang_c = (t % GRID_W).astype(jnp.float32)[:, None] * freqs
    split_idx = [int(i) for i in np.cumsum(IN_SPLITS)[:-1]]
    x_lat, x_ctx = x, ctx
    for l in range(DEPTH):
        with_ctx = l < DEPTH - 1
        lam_init = 0.8 - 0.6 * math.exp(-0.3 * l)
        m_lat = (jax.nn.silu(c) @ ada_w[l] + ada_b[l])[:, None, :]
        m_ctx = jax.nn.silu(c_ctx) @ ada_w[l] + ada_b[l]
        sh1, sc1, g1, sh2, sc2, g2 = jnp.split(m_lat, 6, axis=-1)
        csh1, csc1, cg1, csh2, csc2, cg2 = jnp.split(m_ctx, 6, axis=-1)
        h_lat = rmsnorm(x_lat, norm1_g[l]) * (1.0 + sc1) + sh1
        h_ctx = rmsnorm(x_ctx, norm1_g[l]) * (1.0 + csc1) + csh1
        pl = jnp.split(h_lat @ w_in[l], split_idx, axis=-1)
        pc = jnp.split(h_ctx @ w_in[l], split_idx, axis=-1)
        ys_l, ys_c = ssm_mixer(pl[0], pc[0], ssm_lambda_re[l], ssm_lambda_im[l], ssm_log_step[l],
                               ssm_b_re[l], ssm_b_im[l], ssm_c_re[l], ssm_c_im[l], ssm_d[l], ssm_glu_w[l], with_ctx)
        yd_l, yd_c = diff_mixer(pl[1], pl[2], pl[3], pc[1], pc[2], pc[3], diff_lambda[l], diff_subln_g[l],
                                lam_init, ang_r, ang_c, with_ctx)
        yn_l, yn_c = na_mixer(pl[4], pl[5], pl[6], pc[4], pc[5], pc[6], na_rpb[l], with_ctx)
        mix_l = merge_branches(ys_l, yd_l, yn_l, pl[7], pl[8], pl[9],
                               w_br_ssm[l], w_br_diff[l], w_br_na[l], w_out[l])
        x_lat = x_lat + g1 * mix_l
        h2 = rmsnorm(x_lat, norm2_g[l]) * (1.0 + sc2) + sh2
        x_lat = x_lat + g2 * peer_ffn(h2, peer_wq[l], peer_subkeys[l], peer_u[l], peer_v[l])
        if with_ctx:
            mix_c = merge_branches(ys_c, yd_c, yn_c, pc[7], pc[8], pc[9],
                                   w_br_ssm[l], w_br_diff[l], w_br_na[l], w_out[l])
            x_ctx = x_ctx + cg1 * mix_c
            h2c = rmsnorm(x_ctx, norm2_g[l]) * (1.0 + csc2) + csh2
            x_ctx = x_ctx + cg2 * peer_ffn(h2c, peer_wq[l], peer_subkeys[l], peer_u[l], peer_v[l])
    return rmsnorm(x_lat, final_norm_g)
```

```python
import functools
import math

import numpy as np
import jax
import jax.numpy as jnp
from jax import lax
from jax.experimental import pallas as pl
from jax.experimental.pallas import tpu as pltpu

D = 1024
GRID_W = 64
SSM_W = 256
SSM_G = 16
SSM_P = 16
SSM_N = 64
SSM_RE_MAX = -1e-4
DIFF_H = 4
DIFF_DH = 64
DIFF_W = 512
NA_H = 4
NA_DH = 64
NA_W = 256
NA_WIN_H = 8
NA_WIN_W = 16
ROPE_THETA = 10000.0
ROPE_F = 16
PEER_H = 8
PEER_NK = 128
PEER_E = PEER_NK * PEER_NK
PEER_TOPK = 16
RMS_EPS = 1e-6
NEG_INF = -1e30

VMEM_LIMIT = 56 * 1024 * 1024
TOK_BLK = 256
SSM_TT = 128
ATT_TQ = 512
ATT_TK = 1024
NA_ROWS = 4
NA_UNION = NA_ROWS + NA_WIN_H
RT_BLK = 256
PE_T = 512
PE_EB = 1024
PE_RC = 32
PE_LC = 256

F32 = jnp.float32
BF16 = jnp.bfloat16


def _cparams(sem):
    return pltpu.CompilerParams(dimension_semantics=sem, vmem_limit_bytes=VMEM_LIMIT)


def _const_spec(shape):
    nd = len(shape)
    return pl.BlockSpec(shape, lambda *_: (0,) * nd)


def _rms_mod(x, ng, sc, sh):
    y = x * lax.rsqrt(jnp.mean(x * x, axis=-1, keepdims=True) + RMS_EPS)
    return (y * ng) * (1.0 + sc) + sh


def _gelu(x):
    return 0.5 * x * (1.0 + jnp.tanh(0.7978845608028654 * (x + 0.044715 * (x * x * x))))


def _sigmoid(x):
    return 1.0 / (1.0 + jnp.exp(-x))


def _ada_kernel(c_ref, w_ref, b_ref, o_ref):
    c = c_ref[...]
    s = c * _sigmoid(c)
    o_ref[...] = jnp.dot(s, w_ref[...], preferred_element_type=F32,
                         precision=lax.Precision.HIGHEST) + b_ref[...]


def _ada_mod(c8, w, b, interpret):
    n = w.shape[1]
    tn = 1536
    return pl.pallas_call(
        _ada_kernel,
        grid=(n // tn,),
        in_specs=[pl.BlockSpec((8, D), lambda j: (0, 0)),
                  pl.BlockSpec((D, tn), lambda j: (0, j)),
                  pl.BlockSpec((1, tn), lambda j: (0, j))],
        out_specs=pl.BlockSpec((8, tn), lambda j: (0, j)),
        out_shape=jax.ShapeDtypeStruct((8, n), F32),
        compiler_params=_cparams(("arbitrary",)),
        interpret=interpret, name="ada_mod",
    )(c8, w, b.reshape(1, n))


def _proj_kernel(has_prev, *refs):
    if has_prev:
        (x_ref, pe_ref, ng_ref, mod_ref, cos_ref, sin_ref, wg_ref, wqk_ref, wqs_ref, wr_ref,
         xo_ref, g_ref, qk_ref, a_ref, u_ref) = refs
        x = x_ref[0] + mod_ref[0, 0, 5:6, :] * pe_ref[0]
        xo_ref[0] = x
    else:
        (x_ref, ng_ref, mod_ref, cos_ref, sin_ref, wg_ref, wqk_ref, wqs_ref, wr_ref,
         g_ref, qk_ref, a_ref, u_ref) = refs
        x = x_ref[0]
    mod_ref = mod_ref
    h = _rms_mod(x, ng_ref[...], mod_ref[0, 0, 7:8, :], mod_ref[0, 0, 6:7, :]).astype(BF16)
    g_ref[0] = jnp.dot(h, wg_ref[...], preferred_element_type=F32)
    qk = jnp.dot(h, wqk_ref[...], preferred_element_type=F32)
    qs = jnp.dot(h, wqs_ref[...], preferred_element_type=F32)
    qk_ref[0] = (qk * cos_ref[...] + qs * sin_ref[...]).astype(BF16)
    r = jnp.dot(h, wr_ref[...], preferred_element_type=F32)
    na = DIFF_W + 3 * NA_W
    a_ref[0] = r[:, :na].astype(BF16)
    u_ref[0] = r[:, na:]


def _proj_in(x, peer, ng, mods, cos_t, sin_t, wg, wqk, wqs, wr, L, interpret):
    B, S, _ = x.shape
    T = TOK_BLK
    nl = L // T
    has_prev = peer is not None
    tok = lambda b, i: (b, i, 0)
    tab = lambda b, i: (i, 0)
    in_specs = [pl.BlockSpec((1, T, D), tok)]
    args = [x]
    if has_prev:
        in_specs.append(pl.BlockSpec((1, T, D), tok))
        args.append(peer)
    in_specs += [_const_spec((1, D)),
                 pl.BlockSpec((1, 1, 12, D), lambda b, i: (b, jnp.where(i >= nl, 1, 0), 0, 0)),
                 pl.BlockSpec((T, 2 * DIFF_W), tab), pl.BlockSpec((T, 2 * DIFF_W), tab),
                 _const_spec(wg.shape), _const_spec(wqk.shape), _const_spec(wqs.shape),
                 _const_spec(wr.shape)]
    args += [ng.reshape(1, D), mods, cos_t, sin_t, wg, wqk, wqs, wr]
    na = DIFF_W + 3 * NA_W
    out_specs = [pl.BlockSpec((1, T, 3 * D), tok), pl.BlockSpec((1, T, 2 * DIFF_W), tok),
                 pl.BlockSpec((1, T, na), tok), pl.BlockSpec((1, T, SSM_W), tok)]
    out_shape = [jax.ShapeDtypeStruct((B, S, 3 * D), F32), jax.ShapeDtypeStruct((B, S, 2 * DIFF_W), BF16),
                 jax.ShapeDtypeStruct((B, S, na), BF16), jax.ShapeDtypeStruct((B, S, SSM_W), F32)]
    if has_prev:
        out_specs = [pl.BlockSpec((1, T, D), tok)] + out_specs
        out_shape = [jax.ShapeDtypeStruct((B, S, D), F32)] + out_shape
    outs = pl.pallas_call(
        functools.partial(_proj_kernel, has_prev),
        grid=(B, S // T), in_specs=in_specs, out_specs=out_specs, out_shape=out_shape,
        compiler_params=_cparams(("parallel", "parallel")),
        interpret=interpret, name="proj_in",
    )(*args)
    if has_prev:
        return outs
    return [x] + list(outs)


def _ssm_kernel(uf_ref, ub_ref, bm_ref, are_ref, aim_ref, cm_ref, d_ref, yf_ref, yb_ref,
                lhs_ref, bu_ref, y_ref, st_ref):
    tt = uf_ref.shape[0]
    nb = uf_ref.shape[1]
    hs = SSM_G * SSM_N

    @pl.when(pl.program_id(0) == 0)
    def _():
        st_ref[...] = jnp.zeros_like(st_ref)
        lhs_ref[...] = jnp.zeros_like(lhs_ref)

    lhs_ref[:, 0:nb, 0:SSM_W] = uf_ref[...]
    lhs_ref[:, nb:2 * nb, SSM_W:2 * SSM_W] = ub_ref[...]
    lhs = lhs_ref[...].reshape(tt * 8, 2 * SSM_W).astype(BF16)
    bu_ref[...] = jnp.dot(lhs, bm_ref[...], preferred_element_type=F32)

    def step(t, carry):
        sre, sim = carry
        r = pl.multiple_of(t * 8, 8)
        are = are_ref[...]
        aim = aim_ref[...]
        nre = are * sre - aim * sim + bu_ref[pl.ds(r, 8), 0:hs]
        nim = are * sim + aim * sre + bu_ref[pl.ds(r, 8), hs:2 * hs]
        bu_ref[pl.ds(r, 8), 0:hs] = nre
        bu_ref[pl.ds(r, 8), hs:2 * hs] = nim
        return nre, nim

    sre, sim = lax.fori_loop(0, tt, step, (st_ref[:, 0:hs], st_ref[:, hs:2 * hs]))
    st_ref[:, 0:hs] = sre
    st_ref[:, hs:2 * hs] = sim

    y = jnp.dot(bu_ref[...].astype(BF16), cm_ref[...], preferred_element_type=F32)
    y_ref[...] = y.reshape(tt, 8, 2 * SSM_W)
    yf_ref[...] = y_ref[:, 0:nb, 0:SSM_W] + d_ref[...] * uf_ref[...]
    yb_ref[...] = y_ref[:, nb:2 * nb, SSM_W:2 * SSM_W]


def _ssm_params(lam_re, lam_im, log_step, b_re, b_im, c_re, c_im, nb):
    lam = lax.complex(jnp.minimum(lam_re.astype(F32), SSM_RE_MAX), lam_im.astype(F32))
    lam_dt = lam * jnp.exp(log_step.astype(F32))[..., None]
    lam_bar = jnp.exp(lam_dt)
    b = lax.complex(b_re.astype(F32), b_im.astype(F32))
    b_bar = ((lam_bar - 1.0) / lam)[..., None] * b
    eye = jnp.eye(SSM_G, dtype=F32)
    hs = SSM_G * SSM_N

    def bmat(part):
        return jnp.einsum('gh,dhnp->dgphn', eye, part).reshape(2 * SSM_W, hs)

    bm = jnp.concatenate([bmat(jnp.real(b_bar)), bmat(jnp.imag(b_bar))], axis=1)

    def cmat(part):
        return jnp.einsum('hg,dgpn->hndgp', eye, part).reshape(hs, 2 * SSM_W)

    cm = jnp.concatenate([cmat(c_re.astype(F32)), -cmat(c_im.astype(F32))], axis=0)
    a_re = jnp.repeat(jnp.real(lam_bar).reshape(2, hs), nb, axis=0)
    a_im = jnp.repeat(jnp.imag(lam_bar).reshape(2, hs), nb, axis=0)
    return bm.astype(BF16), a_re, a_im, cm.astype(BF16)


def _ssm(u, L, lam_re, lam_im, log_step, b_re, b_im, c_re, c_im, d, interpret):
    B, S, _ = u.shape
    assert 2 * B == 8, "the scan packs (direction, batch) onto the 8 sublanes"
    bm, a_re, a_im, cm = _ssm_params(lam_re, lam_im, log_step, b_re, b_im, c_re, c_im, B)
    ul, uc = u[:, :L], u[:, L:]
    uf = jnp.transpose(jnp.concatenate([uc, ul], axis=1), (1, 0, 2))
    ub = jnp.transpose(jnp.concatenate([uc[:, ::-1], ul[:, ::-1]], axis=1), (1, 0, 2))
    tt = SSM_TT
    hs2 = 2 * SSM_G * SSM_N
    blk = pl.BlockSpec((tt, B, SSM_W), lambda i: (i, 0, 0))
    yf, yb = pl.pallas_call(
        _ssm_kernel,
        grid=(S // tt,),
        in_specs=[blk, blk, _const_spec(bm.shape), _const_spec(a_re.shape), _const_spec(a_im.shape),
                  _const_spec(cm.shape), _const_spec((1, 1, SSM_W))],
        out_specs=[blk, blk],
        out_shape=[jax.ShapeDtypeStruct((S, B, SSM_W), F32)] * 2,
        scratch_shapes=[pltpu.VMEM((tt, 8, 2 * SSM_W), F32), pltpu.VMEM((tt * 8, hs2), F32),
                        pltpu.VMEM((tt, 8, 2 * SSM_W), F32), pltpu.VMEM((8, hs2), F32)],
        compiler_params=_cparams(("arbitrary",)),
        interpret=interpret, name="ssm_scan",
    )(uf, ub, bm, a_re, a_im, cm, d.astype(F32).reshape(1, 1, SSM_W))
    C = S - L
    y_ctx = yf[:C] + yb[:C][::-1]
    y_lat = yf[C:] + yb[C:][::-1]
    return jnp.transpose(jnp.concatenate([y_lat, y_ctx], axis=0), (1, 0, 2))


def _diff_kernel(with_lat, lam_scale, *refs):
    if with_lat:
        (q_ref, kc_ref, vc_ref, kl_ref, vl_ref, lam_ref, sg_ref, o_ref,
         q1_ref, q2_ref, m_ref, l_ref, acc_ref) = refs
    else:
        (q_ref, kc_ref, vc_ref, lam_ref, sg_ref, yin_ref, o_ref,
         q1_ref, q2_ref, m_ref, l_ref, acc_ref) = refs
    ki = pl.program_id(3)
    nk = pl.num_programs(3)

    def update(j, qm, kt, v):
        s = jnp.dot(qm, kt, preferred_element_type=F32)
        m_prev = m_ref[j]
        m_new = jnp.maximum(m_prev, jnp.max(s, axis=-1, keepdims=True))
        alpha = jnp.exp(m_prev - m_new)
        p = jnp.exp(s - m_new[:, 0:1])
        l_ref[j] = alpha * l_ref[j] + jnp.sum(p, axis=-1, keepdims=True)
        acc_ref[j] = alpha * acc_ref[j] + jnp.dot(p.astype(BF16), v, preferred_element_type=F32)
        m_ref[j] = m_new

    def both(kt, v):
        update(0, q1_ref[...], kt, v)
        update(1, q2_ref[...], kt, v)

    @pl.when(ki == 0)
    def _():
        q = q_ref[0]
        first = lax.broadcasted_iota(jnp.int32, q.shape, 1) < DIFF_DH
        q1_ref[...] = jnp.where(first, q, jnp.zeros_like(q))
        q2_ref[...] = jnp.where(first, jnp.zeros_like(q), q)
        m_ref[...] = jnp.full_like(m_ref, -jnp.inf)
        l_ref[...] = jnp.zeros_like(l_ref)
        acc_ref[...] = jnp.zeros_like(acc_ref)
        both(kc_ref[0], vc_ref[0])

    if with_lat:
        both(kl_ref[0], vl_ref[0])

    @pl.when(ki == nk - 1)
    def _():
        o = acc_ref[0] / l_ref[0] - lam_ref[...] * (acc_ref[1] / l_ref[1])
        y = o * lax.rsqrt(jnp.mean(o * o, axis=-1, keepdims=True) + RMS_EPS)
        o_ref[0] = ((y * sg_ref[...]) * lam_scale).astype(o_ref.dtype)


def _diff_attn(qk, kT, a, lam, subln_g, lam_init, L, y_prev, interpret):
    B, S, _ = qk.shape
    C = S - L
    hw = 2 * DIFF_DH
    lam_v = jnp.broadcast_to(lam.astype(F32).reshape(1, 1), (1, hw))
    sg = subln_g.astype(F32).reshape(1, hw)
    with_lat = y_prev is None
    tq = ATT_TQ if with_lat else C
    tk = min(ATT_TK, L)
    cb = L // C
    kc_spec = pl.BlockSpec((1, hw, C), lambda b, h, qi, ki: (b, h, cb))
    vc_spec = pl.BlockSpec((1, C, hw), lambda b, h, qi, ki: (b, cb, h))
    vec = pl.BlockSpec((1, hw), lambda b, h, qi, ki: (0, 0))
    scratch = [pltpu.VMEM((tq, hw), BF16), pltpu.VMEM((tq, hw), BF16), pltpu.VMEM((2, tq, hw), F32),
               pltpu.VMEM((2, tq, hw), F32), pltpu.VMEM((2, tq, hw), F32)]
    if with_lat:
        grid = (B, DIFF_H, L // tq, L // tk)
        in_specs = [pl.BlockSpec((1, tq, hw), lambda b, h, qi, ki: (b, qi, h)), kc_spec, vc_spec,
                    pl.BlockSpec((1, hw, tk), lambda b, h, qi, ki: (b, h, ki)),
                    pl.BlockSpec((1, tk, hw), lambda b, h, qi, ki: (b, ki, h)), vec, vec]
        args = (qk, kT, a, kT, a, lam_v, sg)
        out_spec = pl.BlockSpec((1, tq, hw), lambda b, h, qi, ki: (b, qi, h))
        aliases = {}
    else:
        grid = (B, DIFF_H, 1, 1)
        in_specs = [pl.BlockSpec((1, C, hw), lambda b, h, qi, ki: (b, cb, h)), kc_spec, vc_spec, vec, vec,
                    pl.BlockSpec(memory_space=pl.ANY)]
        args = (qk, kT, a, lam_v, sg, y_prev)
        out_spec = pl.BlockSpec((1, C, hw), lambda b, h, qi, ki: (b, cb, h))
        aliases = {5: 0}
    return pl.pallas_call(
        functools.partial(_diff_kernel, with_lat, 1.0 - lam_init),
        grid=grid, in_specs=in_specs, out_specs=out_spec,
        out_shape=jax.ShapeDtypeStruct((B, S, DIFF_W), BF16),
        scratch_shapes=scratch, input_output_aliases=aliases,
        compiler_params=_cparams(("parallel", "parallel", "parallel", "arbitrary")),
        interpret=interpret, name="diff_attn_lat" if with_lat else "diff_attn_ctx",
    )(*args)


def _na_bias_tables(rpb, rows):
    W = GRID_W
    qn, kn = NA_ROWS * W, NA_UNION * W
    a = np.arange(qn) // W
    qc = np.arange(qn) % W
    j = np.arange(kn) // W
    kc = np.arange(kn) % W
    cstart = np.clip(qc - NA_WIN_W // 2, 0, W - NA_WIN_W)
    col_ok = (kc[None, :] >= cstart[:, None]) & (kc[None, :] < cstart[:, None] + NA_WIN_W)
    dc = np.clip(kc[None, :] - qc[:, None] + NA_WIN_W - 1, 0, 2 * NA_WIN_W - 2)
    tabs = []
    for r0 in (0, NA_ROWS * 2, rows - NA_ROWS):
        u0 = int(np.clip(r0 - NA_WIN_H // 2, 0, rows - NA_UNION))
        r = r0 + a
        kr = u0 + j
        start = np.clip(r - NA_WIN_H // 2, 0, rows - NA_WIN_H)
        row_ok = (kr[None, :] >= start[:, None]) & (kr[None, :] < start[:, None] + NA_WIN_H)
        dr = np.clip(kr[None, :] - r[:, None] + NA_WIN_H - 1, 0, 2 * NA_WIN_H - 2)
        ok = row_ok & col_ok
        tabs.append(jnp.where(ok[None], rpb.astype(F32)[:, dr, dc], NEG_INF))
    return jnp.stack(tabs)


def _na_kernel(rows, q_ref, k_ref, v_ref, bias_ref, o_ref):
    i = pl.program_id(1)
    L = rows * GRID_W
    u0 = jnp.clip(i * NA_ROWS - NA_WIN_H // 2, 0, rows - NA_UNION)
    ks = pl.multiple_of(u0 * GRID_W, GRID_W)
    nloc = NA_UNION * GRID_W
    q = q_ref[0]
    kw = k_ref[0, pl.ds(ks, nloc), :]
    vw = v_ref[0, pl.ds(ks, nloc), :]
    kc = k_ref[0, L:, :]
    vc = v_ref[0, L:, :]
    dn = (((1,), (1,)), ((), ()))
    outs = []
    for h in range(NA_H):
        sl = slice(h * NA_DH, (h + 1) * NA_DH)
        qh = q[:, sl]
        b = bias_ref[0, h]
        s_loc = lax.dot_general(qh, kw[:, sl], dn, preferred_element_type=F32)
        s_loc = jnp.where(b > 0.5 * NEG_INF, s_loc + b, NEG_INF)
        s_ctx = lax.dot_general(qh, kc[:, sl], dn, preferred_element_type=F32)
        m = jnp.maximum(jnp.max(s_loc, axis=-1, keepdims=True), jnp.max(s_ctx, axis=-1, keepdims=True))
        p_loc = jnp.exp(s_loc - m)
        p_ctx = jnp.exp(s_ctx - m)
        den = jnp.sum(p_loc, axis=-1, keepdims=True) + jnp.sum(p_ctx, axis=-1, keepdims=True)
        o = (jnp.dot(p_loc.astype(BF16), vw[:, sl], preferred_element_type=F32)
             + jnp.dot(p_ctx.astype(BF16), vc[:, sl], preferred_element_type=F32))
        outs.append(o / den)
    o_ref[0] = jnp.concatenate(outs, axis=-1).astype(o_ref.dtype)


def _na_ctx_kernel(q_ref, k_ref, v_ref, yin_ref, o_ref):
    q = q_ref[0]
    k = k_ref[0]
    v = v_ref[0]
    dn = (((1,), (1,)), ((), ()))
    outs = []
    for h in range(NA_H):
        sl = slice(h * NA_DH, (h + 1) * NA_DH)
        s = lax.dot_general(q[:, sl], k[:, sl], dn, preferred_element_type=F32)
        p = jnp.exp(s - jnp.max(s, axis=-1, keepdims=True))
        o = jnp.dot(p.astype(BF16), v[:, sl], preferred_element_type=F32)
        outs.append(o / jnp.sum(p, axis=-1, keepdims=True))
    o_ref[0] = jnp.concatenate(outs, axis=-1).astype(o_ref.dtype)


def _na_attn(a, rpb, L, with_ctx, interpret):
    B, S, _ = a.shape
    C = S - L
    rows = L // GRID_W
    assert rows >= NA_UNION and rows % NA_ROWS == 0
    nb = rows // NA_ROWS
    qn, kn = NA_ROWS * GRID_W, NA_UNION * GRID_W
    bias = _na_bias_tables(rpb, rows)
    qcol, kcol, vcol = DIFF_W // NA_W, DIFF_W // NA_W + 1, DIFF_W // NA_W + 2
    y = pl.pallas_call(
        functools.partial(_na_kernel, rows),
        grid=(B, nb),
        in_specs=[pl.BlockSpec((1, qn, NA_W), lambda b, i: (b, i, qcol)),
                  pl.BlockSpec((1, S, NA_W), lambda b, i: (b, 0, kcol)),
                  pl.BlockSpec((1, S, NA_W), lambda b, i: (b, 0, vcol)),
                  pl.BlockSpec((1, NA_H, qn, kn),
                               lambda b, i: (jnp.where(i == 0, 0, jnp.where(i == nb - 1, 2, 1)), 0, 0, 0))],
        out_specs=pl.BlockSpec((1, qn, NA_W), lambda b, i: (b, i, 0)),
        out_shape=jax.ShapeDtypeStruct((B, S, NA_W), BF16),
        compiler_params=_cparams(("parallel", "arbitrary")),
        interpret=interpret, name="na_attn_lat",
    )(a, a, a, bias)
    if not with_ctx:
        return y
    cb = L // C
    return pl.pallas_call(
        _na_ctx_kernel,
        grid=(B,),
        in_specs=[pl.BlockSpec((1, C, NA_W), lambda b: (b, cb, qcol)),
                  pl.BlockSpec((1, C, NA_W), lambda b: (b, cb, kcol)),
                  pl.BlockSpec((1, C, NA_W), lambda b: (b, cb, vcol)),
                  pl.BlockSpec(memory_space=pl.ANY)],
        out_specs=pl.BlockSpec((1, C, NA_W), lambda b: (b, cb, 0)),
        out_shape=jax.ShapeDtypeStruct((B, S, NA_W), BF16),
        input_output_aliases={3: 0},
        compiler_params=_cparams(("parallel",)),
        interpret=interpret, name="na_attn_ctx",
    )(a, a, a, y)


def _merge_kernel(x_ref, g_ref, ys_ref, yd_ref, yn_ref, mod_ref, ng_ref, wglu_ref, wbs_ref, wbd_ref,
                  wbn_ref, wo_ref, wq_ref, xo_ref, h2t_ref, qp_ref):
    ys = _gelu(ys_ref[0]).astype(BF16)
    z = jnp.dot(ys, wglu_ref[...], preferred_element_type=F32)
    ssm = (z[:, :SSM_W] * _sigmoid(z[:, SSM_W:])).astype(BF16)
    g = g_ref[0]
    m = (_sigmoid(g[:, 0:D]) * jnp.dot(ssm, wbs_ref[...], preferred_element_type=F32)
         + _sigmoid(g[:, D:2 * D]) * jnp.dot(yd_ref[0], wbd_ref[...], preferred_element_type=F32)
         + _sigmoid(g[:, 2 * D:3 * D]) * jnp.dot(yn_ref[0], wbn_ref[...], preferred_element_type=F32))
    mix = jnp.dot(m.astype(BF16), wo_ref[...], preferred_element_type=F32)
    x = x_ref[0] + mod_ref[0, 0, 2:3, :] * mix
    xo_ref[0] = x
    h2 = _rms_mod(x, ng_ref[...], mod_ref[0, 0, 4:5, :], mod_ref[0, 0, 3:4, :])
    h2t_ref[...] = h2.T.astype(BF16)
    q = jnp.dot(h2.astype(BF16), wq_ref[...], preferred_element_type=F32)
    for j in range(2 * PEER_H):
        qp_ref[j] = q[:, j * PEER_NK:(j + 1) * PEER_NK]


def _merge(x, g, ys, yd, yn, mods, ng2, wglu, wbs, wbd, wbn, wo, wq, L, sq, interpret):
    B, S, _ = x.shape
    T = TOK_BLK
    nl = L // T
    nt = sq // T
    tok = lambda b, i: (b, i, 0)
    ws = [wglu, wbs, wbd, wbn, wo, wq]
    return pl.pallas_call(
        _merge_kernel,
        grid=(B, nt),
        in_specs=[pl.BlockSpec((1, T, D), tok), pl.BlockSpec((1, T, 3 * D), tok),
                  pl.BlockSpec((1, T, SSM_W), tok), pl.BlockSpec((1, T, DIFF_W), tok),
                  pl.BlockSpec((1, T, NA_W), tok),
                  pl.BlockSpec((1, 1, 6, D), lambda b, i: (b, jnp.where(i >= nl, 1, 0), 0, 0)),
                  _const_spec((1, D))] + [_const_spec(w.shape) for w in ws],
        out_specs=[pl.BlockSpec((1, T, D), tok),
                   pl.BlockSpec((D, T), lambda b, i: (0, b * nt + i)),
                   pl.BlockSpec((2 * PEER_H, T, PEER_NK), lambda b, i: (0, b * nt + i, 0))],
        out_shape=[jax.ShapeDtypeStruct((B, sq, D), F32), jax.ShapeDtypeStruct((D, B * sq), BF16),
                   jax.ShapeDtypeStruct((2 * PEER_H, B * sq, PEER_NK), F32)],
        compiler_params=_cparams(("parallel", "parallel")),
        interpret=interpret, name="merge_branches",
    )(x, g, ys, yd, yn, mods, ng2.reshape(1, D), *ws)


def _route_kernel(q_ref, sk_ref, s0_ref, s1_ref, st_ref, v0_ref, v1_ref):
    dn = (((1,), (1,)), ((), ()))
    ninf = jnp.float32(-jnp.inf)

    def top_rows(s, v_ref):
        for r in range(PEER_TOPK):
            m = jnp.max(s, axis=0, keepdims=True)
            v_ref[r:r + 1, :] = m
            s = jnp.where(s == m, ninf, s)

    def head(h, _):
        s0 = lax.dot_general(sk_ref[2 * h], q_ref[2 * h].astype(BF16), dn, preferred_element_type=F32)
        s1 = lax.dot_general(sk_ref[2 * h + 1], q_ref[2 * h + 1].astype(BF16), dn, preferred_element_type=F32)
        s0_ref[h] = s0
        s1_ref[h] = s1
        top_rows(s0, v0_ref)
        top_rows(s1, v1_ref)
        v1a = v1_ref[...]
        v1h = v1_ref[0:8, :]
        parts = [v0_ref[0:1, :] + v1a]
        for r in range(1, 8):
            parts.append(v0_ref[r:r + 1, :] + v1h)
        parts.append(v0_ref[8:16, :] + v1_ref[0:1, :])
        c = jnp.concatenate(parts, axis=0)
        cmax = v0_ref[0:1, :] + v1_ref[0:1, :]
        z = jnp.zeros_like(cmax)
        m = cmax
        for _k in range(PEER_TOPK):
            m = jnp.max(c, axis=0, keepdims=True)
            z = z + jnp.exp(m - cmax)
            c = jnp.where(c == m, ninf, c)
        st_ref[h] = jnp.concatenate([m, v0_ref[0:1, :], v1_ref[0:1, :], 1.0 / z,
                                     jnp.zeros((4, m.shape[1]), F32)], axis=0)
        return 0

    lax.fori_loop(0, PEER_H, head, 0)


def _route(qp, sk, interpret):
    ntok = qp.shape[1]
    T = RT_BLK
    sspec = pl.BlockSpec((PEER_H, PEER_NK, T), lambda t: (0, 0, t))
    return pl.pallas_call(
        _route_kernel,
        grid=(ntok // T,),
        in_specs=[pl.BlockSpec((2 * PEER_H, T, PEER_NK), lambda t: (0, t, 0)), _const_spec(sk.shape)],
        out_specs=[sspec, sspec, pl.BlockSpec((PEER_H, 8, T), lambda t: (0, 0, t))],
        out_shape=[jax.ShapeDtypeStruct((PEER_H, PEER_NK, ntok), F32)] * 2
                  + [jax.ShapeDtypeStruct((PEER_H, 8, ntok), F32)],
        scratch_shapes=[pltpu.VMEM((PEER_TOPK, T), F32), pltpu.VMEM((PEER_TOPK, T), F32)],
        compiler_params=_cparams(("parallel",)),
        interpret=interpret, name="peer_route",
    )(qp, sk)


def _peer_kernel(h_ref, u_ref, vt_ref, s0_ref, s1_ref, st_ref, o_ref, e0_ref, e1_ref, act_ref, wt_ref,
                 acc_ref):
    e = pl.program_id(1)
    ni = PE_EB // PEER_NK

    @pl.when(e == 0)
    def _():
        acc_ref[...] = jnp.zeros_like(acc_ref)
        for h in range(PEER_H):
            e0_ref[h] = jnp.exp(s0_ref[h] - st_ref[h, 1:2, :]) * st_ref[h, 3:4, :]
            e1_ref[h] = jnp.exp(s1_ref[h] - st_ref[h, 2:3, :])

    act_ref[...] = jnp.dot(u_ref[...], h_ref[...], preferred_element_type=F32)

    def chunk(n, _):
        ii = n // (PEER_NK // PE_RC)
        jc = n % (PEER_NK // PE_RC)
        i = e * ni + ii
        jr = pl.multiple_of(jc * PE_RC, PE_RC)
        er = pl.multiple_of(ii * PEER_NK + jc * PE_RC, PE_RC)
        for lt in range(PE_T // PE_LC):
            ls = slice(lt * PE_LC, (lt + 1) * PE_LC)
            g = jnp.zeros((PE_RC, PE_LC), F32)
            for h in range(PEER_H):
                c = s0_ref[h, pl.ds(i, 1), ls] + s1_ref[h, pl.ds(jr, PE_RC), ls]
                w = e0_ref[h, pl.ds(i, 1), ls] * e1_ref[h, pl.ds(jr, PE_RC), ls]
                g = g + jnp.where(c >= st_ref[h, 0:1, ls], w, 0.0)
            wt_ref[pl.ds(er, PE_RC), ls] = (g * _gelu(act_ref[pl.ds(er, PE_RC), ls])).astype(BF16)
        return 0

    lax.fori_loop(0, ni * (PEER_NK // PE_RC), chunk, 0)
    acc_ref[...] += jnp.dot(vt_ref[...], wt_ref[...], preferred_element_type=F32)

    @pl.when(e == pl.num_programs(1) - 1)
    def _():
        o_ref[...] = acc_ref[...].T


def _peer(h2t, u_b, vt_b, s0, s1, st, interpret):
    ntok = h2t.shape[1]
    T, EB = PE_T, PE_EB
    sspec = pl.BlockSpec((PEER_H, PEER_NK, T), lambda t, e: (0, 0, t))
    return pl.pallas_call(
        _peer_kernel,
        grid=(ntok // T, PEER_E // EB),
        in_specs=[pl.BlockSpec((D, T), lambda t, e: (0, t)),
                  pl.BlockSpec((EB, D), lambda t, e: (e, 0)),
                  pl.BlockSpec((D, EB), lambda t, e: (0, e)),
                  sspec, sspec, pl.BlockSpec((PEER_H, 8, T), lambda t, e: (0, 0, t))],
        out_specs=pl.BlockSpec((T, D), lambda t, e: (t, 0)),
        out_shape=jax.ShapeDtypeStruct((ntok, D), F32),
        scratch_shapes=[pltpu.VMEM((PEER_H, PEER_NK, T), F32), pltpu.VMEM((PEER_H, PEER_NK, T), F32),
                        pltpu.VMEM((EB, T), F32), pltpu.VMEM((EB, T), BF16), pltpu.VMEM((D, T), F32)],
        compiler_params=_cparams(("parallel", "arbitrary")),
        interpret=interpret, name="peer_dense",
    )(h2t, u_b, vt_b, s0, s1, st)


def _final_kernel(x_ref, pe_ref, g2_ref, ng_ref, o_ref):
    x = x_ref[0] + g2_ref[0] * pe_ref[0]
    y = x * lax.rsqrt(jnp.mean(x * x, axis=-1, keepdims=True) + RMS_EPS)
    o_ref[0] = y * ng_ref[...]


def _final(x, peer, g2, ng, interpret):
    B, L, _ = x.shape
    T = TOK_BLK
    tok = lambda b, i: (b, i, 0)
    return pl.pallas_call(
        _final_kernel,
        grid=(B, L // T),
        in_specs=[pl.BlockSpec((1, T, D), tok), pl.BlockSpec((1, T, D), tok),
                  pl.BlockSpec((1, 1, D), lambda b, i: (b, 0, 0)), _const_spec((1, D))],
        out_specs=pl.BlockSpec((1, T, D), tok),
        out_shape=jax.ShapeDtypeStruct((B, L, D), F32),
        compiler_params=_cparams(("parallel", "parallel")),
        interpret=interpret, name="final_norm",
    )(x, peer, g2, ng.reshape(1, D))


def _rope_tables(L, C):
    t = jnp.arange(L)
    freqs = ROPE_THETA ** (-jnp.arange(ROPE_F, dtype=F32) / ROPE_F)
    ang_r = (t // GRID_W).astype(F32)[:, None] * freqs
    ang_c = (t % GRID_W).astype(F32)[:, None] * freqs
    cos64 = jnp.concatenate([jnp.cos(ang_r)] * 2 + [jnp.cos(ang_c)] * 2, axis=-1)
    sin64 = jnp.concatenate([-jnp.sin(ang_r), jnp.sin(ang_r), -jnp.sin(ang_c), jnp.sin(ang_c)], axis=-1)
    reps = DIFF_W // DIFF_DH
    scale = DIFF_DH ** -0.5
    cos_l = jnp.concatenate([jnp.tile(cos64, (1, reps)) * scale, jnp.tile(cos64, (1, reps))], axis=-1)
    sin_l = jnp.concatenate([jnp.tile(sin64, (1, reps)) * scale, jnp.tile(sin64, (1, reps))], axis=-1)
    cos_c = jnp.concatenate([jnp.full((C, DIFF_W), scale, F32), jnp.ones((C, DIFF_W), F32)], axis=-1)
    return (jnp.concatenate([cos_l, cos_c], axis=0),
            jnp.concatenate([sin_l, jnp.zeros((C, 2 * DIFF_W), F32)], axis=0))


def _split_w_in(w):
    o = np.cumsum([0, SSM_W, DIFF_W, DIFF_W, DIFF_W, NA_W, NA_W, NA_W, D, D, D])
    seg = lambda k: w[:, o[k]:o[k + 1]]
    wg = jnp.concatenate([seg(7), seg(8), seg(9)], axis=1)
    wqk = jnp.concatenate([seg(1), seg(2)], axis=1)
    j = np.arange(2 * DIFF_W)
    partner = np.where(j % 32 < ROPE_F, j + ROPE_F, j - ROPE_F)
    wqs = wqk[:, partner]
    wr = jnp.concatenate([seg(3), seg(4) * (NA_DH ** -0.5), seg(5), seg(6), seg(0)], axis=1)
    return wg.astype(BF16), wqk.astype(BF16), wqs.astype(BF16), wr.astype(BF16)


def _forward(x, c, ctx, c_ctx, ada_w, ada_b, norm1_g, norm2_g, w_in, ssm_lambda_re, ssm_lambda_im,
             ssm_log_step, ssm_b_re, ssm_b_im, ssm_c_re, ssm_c_im, ssm_d, ssm_glu_w, diff_lambda,
             diff_subln_g, na_rpb, w_br_ssm, w_br_diff, w_br_na, w_out, peer_wq, peer_subkeys, peer_u,
             peer_v, final_norm_g, interpret=False):
    B, L, _ = x.shape
    C = ctx.shape[1]
    S = L + C
    depth = ada_w.shape[0]
    assert C == TOK_BLK and L % ATT_TQ == 0 and L % C == 0
    xs = jnp.concatenate([x, ctx], axis=1).astype(F32)
    c8 = jnp.concatenate([c, c_ctx[None], jnp.zeros((8 - B - 1, D), c.dtype)], axis=0).astype(F32)
    cos_t, sin_t = _rope_tables(L, C)

    def layer_mods(l):
        m = _ada_mod(c8, ada_w[l].astype(F32), ada_b[l].astype(F32), interpret).reshape(8, 6, D)
        lat = m[:B]
        return jnp.stack([lat, jnp.broadcast_to(m[B][None], lat.shape)], axis=1)

    peer_out = None
    prev_mods = None
    for l in range(depth):
        with_ctx = l < depth - 1
        lam_init = 0.8 - 0.6 * math.exp(-0.3 * l)
        mods = layer_mods(l)
        both = jnp.concatenate([prev_mods if prev_mods is not None else jnp.zeros_like(mods), mods], axis=2)
        wg, wqk, wqs, wr = _split_w_in(w_in[l])
        xs, g, qk, a, u = _proj_in(xs, peer_out, norm1_g[l].astype(F32), both, cos_t, sin_t,
                                   wg, wqk, wqs, wr, L, interpret)
        ys = _ssm(u, L, ssm_lambda_re[l], ssm_lambda_im[l], ssm_log_step[l], ssm_b_re[l], ssm_b_im[l],
                  ssm_c_re[l], ssm_c_im[l], ssm_d[l], interpret)
        lq = diff_lambda[l].astype(F32)
        lam = jnp.exp(jnp.sum(lq[0] * lq[1])) - jnp.exp(jnp.sum(lq[2] * lq[3])) + lam_init
        kT = jnp.transpose(qk[:, :, DIFF_W:], (0, 2, 1))
        yd = _diff_attn(qk, kT, a, lam, diff_subln_g[l], lam_init, L, None, interpret)
        if with_ctx:
            yd = _diff_attn(qk, kT, a, lam, diff_subln_g[l], lam_init, L, yd, interpret)
        yn = _na_attn(a, na_rpb[l], L, with_ctx, interpret)
        sq = S if with_ctx else L
        xs2, h2t, qp = _merge(xs, g, ys, yd, yn, mods, norm2_g[l].astype(F32),
                              ssm_glu_w[l].astype(BF16), w_br_ssm[l].astype(BF16),
                              w_br_diff[l].astype(BF16), w_br_na[l].astype(BF16), w_out[l].astype(BF16),
                              peer_wq[l].astype(BF16), L, sq, interpret)
        sk = peer_subkeys[l].astype(BF16).reshape(2 * PEER_H, PEER_NK, -1)
        s0, s1, st = _route(qp, sk, interpret)
        pe = _peer(h2t, peer_u[l].astype(BF16), jnp.transpose(peer_v[l].astype(BF16)), s0, s1, st, interpret)
        peer_out = pe.reshape(B, sq, D)
        xs = xs2
        prev_mods = mods
    g2 = prev_mods[:, 0, 5:6, :]
    return _final(xs, peer_out, g2, final_norm_g.astype(F32), interpret)


def kernel(x, c, ctx, c_ctx, ada_w, ada_b, norm1_g, norm2_g, w_in, ssm_lambda_re, ssm_lambda_im, ssm_log_step, ssm_b_re, ssm_b_im, ssm_c_re, ssm_c_im, ssm_d, ssm_glu_w, diff_lambda, diff_subln_g, na_rpb, w_br_ssm, w_br_diff, w_br_na, w_out, peer_wq, peer_subkeys, peer_u, peer_v, final_norm_g):
    return _forward(x, c, ctx, c_ctx, ada_w, ada_b, norm1_g, norm2_g, w_in, ssm_lambda_re, ssm_lambda_im,
                    ssm_log_step, ssm_b_re, ssm_b_im, ssm_c_re, ssm_c_im, ssm_d, ssm_glu_w, diff_lambda,
                    diff_subln_g, na_rpb, w_br_ssm, w_br_diff, w_br_na, w_out, peer_wq, peer_subkeys,
                    peer_u, peer_v, final_norm_g)
```

```python
import functools
import math

import numpy as np
import jax
import jax.numpy as jnp
from jax import lax
from jax.experimental import pallas as pl
from jax.experimental.pallas import tpu as pltpu

D = 1024
GRID_W = 64
SSM_W = 256
SSM_G = 16
SSM_P = 16
SSM_N = 64
SSM_RE_MAX = -1e-4
DIFF_H = 4
DIFF_DH = 64
DIFF_W = 512
NA_H = 4
NA_DH = 64
NA_W = 256
NA_WIN_H = 8
NA_WIN_W = 16
ROPE_THETA = 10000.0
ROPE_F = 16
PEER_H = 8
PEER_NK = 128
PEER_E = PEER_NK * PEER_NK
PEER_TOPK = 16
RMS_EPS = 1e-6
NEG_INF = -1e30

VMEM_LIMIT = 56 * 1024 * 1024
TOK_BLK = 256
SSM_TT = 128
ATT_TQ = 512
ATT_TK = 1024
NA_ROWS = 4
NA_UNION = NA_ROWS + NA_WIN_H
RT_BLK = 256
PE_T = 512
PE_EB = 1024
PE_RC = 64
PE_LC = 256

F32 = jnp.float32
BF16 = jnp.bfloat16


def _cparams(sem, flags=None):
    return pltpu.CompilerParams(dimension_semantics=sem, vmem_limit_bytes=VMEM_LIMIT, flags=flags)


def _const_spec(shape):
    nd = len(shape)
    return pl.BlockSpec(shape, lambda *_: (0,) * nd)


def _rms_mod(x, ng, sc, sh):
    y = x * lax.rsqrt(jnp.mean(x * x, axis=-1, keepdims=True) + RMS_EPS)
    return (y * ng) * (1.0 + sc) + sh


def _gelu(x):
    return 0.5 * x * (1.0 + jnp.tanh(0.7978845608028654 * (x + 0.044715 * (x * x * x))))


def _sigmoid(x):
    return 1.0 / (1.0 + jnp.exp(-x))


def _ada_kernel(c_ref, w_ref, b_ref, o_ref):
    c = c_ref[...]
    s = c * _sigmoid(c)
    o_ref[...] = jnp.dot(s, w_ref[...], preferred_element_type=F32,
                         precision=lax.Precision.HIGHEST) + b_ref[...]


def _ada_mod(c8, w, b, interpret):
    n = w.shape[1]
    tn = 1536
    return pl.pallas_call(
        _ada_kernel,
        grid=(n // tn,),
        in_specs=[pl.BlockSpec((8, D), lambda j: (0, 0)),
                  pl.BlockSpec((D, tn), lambda j: (0, j)),
                  pl.BlockSpec((1, tn), lambda j: (0, j))],
        out_specs=pl.BlockSpec((8, tn), lambda j: (0, j)),
        out_shape=jax.ShapeDtypeStruct((8, n), F32),
        compiler_params=_cparams(("arbitrary",)),
        interpret=interpret, name="ada_mod",
    )(c8, w, b.reshape(1, n))


def _proj_kernel(has_prev, *refs):
    if has_prev:
        (x_ref, pe_ref, ng_ref, mod_ref, cos_ref, sin_ref, wg_ref, wqk_ref, wqs_ref, wr_ref,
         xo_ref, g_ref, qk_ref, a_ref, u_ref) = refs
        x = x_ref[0] + mod_ref[0, 0, 5:6, :] * pe_ref[0]
        xo_ref[0] = x
    else:
        (x_ref, ng_ref, mod_ref, cos_ref, sin_ref, wg_ref, wqk_ref, wqs_ref, wr_ref,
         g_ref, qk_ref, a_ref, u_ref) = refs
        x = x_ref[0]
    mod_ref = mod_ref
    h = _rms_mod(x, ng_ref[...], mod_ref[0, 0, 7:8, :], mod_ref[0, 0, 6:7, :]).astype(BF16)
    g_ref[0] = jnp.dot(h, wg_ref[...], preferred_element_type=F32)
    qk = jnp.dot(h, wqk_ref[...], preferred_element_type=F32)
    qs = jnp.dot(h, wqs_ref[...], preferred_element_type=F32)
    qk_ref[0] = (qk * cos_ref[...] + qs * sin_ref[...]).astype(BF16)
    r = jnp.dot(h, wr_ref[...], preferred_element_type=F32)
    na = DIFF_W + 3 * NA_W
    a_ref[0] = r[:, :na].astype(BF16)
    u_ref[0] = r[:, na:]


def _proj_in(x, peer, ng, mods, cos_t, sin_t, wg, wqk, wqs, wr, L, interpret):
    B, S, _ = x.shape
    T = TOK_BLK
    nl = L // T
    has_prev = peer is not None
    tok = lambda b, i: (b, i, 0)
    tab = lambda b, i: (i, 0)
    in_specs = [pl.BlockSpec((1, T, D), tok)]
    args = [x]
    if has_prev:
        in_specs.append(pl.BlockSpec((1, T, D), tok))
        args.append(peer)
    in_specs += [_const_spec((1, D)),
                 pl.BlockSpec((1, 1, 12, D), lambda b, i: (b, jnp.where(i >= nl, 1, 0), 0, 0)),
                 pl.BlockSpec((T, 2 * DIFF_W), tab), pl.BlockSpec((T, 2 * DIFF_W), tab),
                 _const_spec(wg.shape), _const_spec(wqk.shape), _const_spec(wqs.shape),
                 _const_spec(wr.shape)]
    args += [ng.reshape(1, D), mods, cos_t, sin_t, wg, wqk, wqs, wr]
    na = DIFF_W + 3 * NA_W
    out_specs = [pl.BlockSpec((1, T, 3 * D), tok), pl.BlockSpec((1, T, 2 * DIFF_W), tok),
                 pl.BlockSpec((1, T, na), tok), pl.BlockSpec((1, T, SSM_W), tok)]
    out_shape = [jax.ShapeDtypeStruct((B, S, 3 * D), F32), jax.ShapeDtypeStruct((B, S, 2 * DIFF_W), BF16),
                 jax.ShapeDtypeStruct((B, S, na), BF16), jax.ShapeDtypeStruct((B, S, SSM_W), F32)]
    if has_prev:
        out_specs = [pl.BlockSpec((1, T, D), tok)] + out_specs
        out_shape = [jax.ShapeDtypeStruct((B, S, D), F32)] + out_shape
    outs = pl.pallas_call(
        functools.partial(_proj_kernel, has_prev),
        grid=(B, S // T), in_specs=in_specs, out_specs=out_specs, out_shape=out_shape,
        compiler_params=_cparams(("parallel", "parallel")),
        interpret=interpret, name="proj_in",
    )(*args)
    if has_prev:
        return outs
    return [x] + list(outs)


def _ssm_kernel(uf_ref, ub_ref, bm_ref, are_ref, aim_ref, cm_ref, d_ref, yf_ref, yb_ref,
                lhs_ref, bu_ref, y_ref, st_ref):
    tt = uf_ref.shape[0]
    nb = uf_ref.shape[1]
    hs = SSM_G * SSM_N

    @pl.when(pl.program_id(0) == 0)
    def _():
        st_ref[...] = jnp.zeros_like(st_ref)
        lhs_ref[...] = jnp.zeros_like(lhs_ref)

    lhs_ref[:, 0:nb, 0:SSM_W] = uf_ref[...]
    lhs_ref[:, nb:2 * nb, SSM_W:2 * SSM_W] = ub_ref[...]
    lhs = lhs_ref[...].reshape(tt * 8, 2 * SSM_W).astype(BF16)
    bu_ref[...] = jnp.dot(lhs, bm_ref[...], preferred_element_type=F32)

    def step(t, carry):
        sre, sim = carry
        r = pl.multiple_of(t * 8, 8)
        are = are_ref[...]
        aim = aim_ref[...]
        nre = are * sre - aim * sim + bu_ref[pl.ds(r, 8), 0:hs]
        nim = are * sim + aim * sre + bu_ref[pl.ds(r, 8), hs:2 * hs]
        bu_ref[pl.ds(r, 8), 0:hs] = nre
        bu_ref[pl.ds(r, 8), hs:2 * hs] = nim
        return nre, nim

    sre, sim = lax.fori_loop(0, tt, step, (st_ref[:, 0:hs], st_ref[:, hs:2 * hs]))
    st_ref[:, 0:hs] = sre
    st_ref[:, hs:2 * hs] = sim

    y = jnp.dot(bu_ref[...].astype(BF16), cm_ref[...], preferred_element_type=F32)
    y_ref[...] = y.reshape(tt, 8, 2 * SSM_W)
    yf_ref[...] = y_ref[:, 0:nb, 0:SSM_W] + d_ref[...] * uf_ref[...]
    yb_ref[...] = y_ref[:, nb:2 * nb, SSM_W:2 * SSM_W]


def _ssm_params(lam_re, lam_im, log_step, b_re, b_im, c_re, c_im, nb):
    lam = lax.complex(jnp.minimum(lam_re.astype(F32), SSM_RE_MAX), lam_im.astype(F32))
    lam_dt = lam * jnp.exp(log_step.astype(F32))[..., None]
    lam_bar = jnp.exp(lam_dt)
    b = lax.complex(b_re.astype(F32), b_im.astype(F32))
    b_bar = ((lam_bar - 1.0) / lam)[..., None] * b
    eye = jnp.eye(SSM_G, dtype=F32)
    hs = SSM_G * SSM_N

    def bmat(part):
        return jnp.einsum('gh,dhnp->dgphn', eye, part).reshape(2 * SSM_W, hs)

    bm = jnp.concatenate([bmat(jnp.real(b_bar)), bmat(jnp.imag(b_bar))], axis=1)

    def cmat(part):
        return jnp.einsum('hg,dgpn->hndgp', eye, part).reshape(hs, 2 * SSM_W)

    cm = jnp.concatenate([cmat(c_re.astype(F32)), -cmat(c_im.astype(F32))], axis=0)
    a_re = jnp.repeat(jnp.real(lam_bar).reshape(2, hs), nb, axis=0)
    a_im = jnp.repeat(jnp.imag(lam_bar).reshape(2, hs), nb, axis=0)
    return bm.astype(BF16), a_re, a_im, cm.astype(BF16)


def _ssm(u, L, lam_re, lam_im, log_step, b_re, b_im, c_re, c_im, d, interpret):
    B, S, _ = u.shape
    assert 2 * B == 8, "the scan packs (direction, batch) onto the 8 sublanes"
    bm, a_re, a_im, cm = _ssm_params(lam_re, lam_im, log_step, b_re, b_im, c_re, c_im, B)
    ul, uc = u[:, :L], u[:, L:]
    uf = jnp.transpose(jnp.concatenate([uc, ul], axis=1), (1, 0, 2))
    ub = jnp.transpose(jnp.concatenate([uc[:, ::-1], ul[:, ::-1]], axis=1), (1, 0, 2))
    tt = SSM_TT
    hs2 = 2 * SSM_G * SSM_N
    blk = pl.BlockSpec((tt, B, SSM_W), lambda i: (i, 0, 0))
    yf, yb = pl.pallas_call(
        _ssm_kernel,
        grid=(S // tt,),
        in_specs=[blk, blk, _const_spec(bm.shape), _const_spec(a_re.shape), _const_spec(a_im.shape),
                  _const_spec(cm.shape), _const_spec((1, 1, SSM_W))],
        out_specs=[blk, blk],
        out_shape=[jax.ShapeDtypeStruct((S, B, SSM_W), F32)] * 2,
        scratch_shapes=[pltpu.VMEM((tt, 8, 2 * SSM_W), F32), pltpu.VMEM((tt * 8, hs2), F32),
                        pltpu.VMEM((tt, 8, 2 * SSM_W), F32), pltpu.VMEM((8, hs2), F32)],
        compiler_params=_cparams(("arbitrary",)),
        interpret=interpret, name="ssm_scan",
    )(uf, ub, bm, a_re, a_im, cm, d.astype(F32).reshape(1, 1, SSM_W))
    C = S - L
    y_ctx = yf[:C] + yb[:C][::-1]
    y_lat = yf[C:] + yb[C:][::-1]
    return jnp.transpose(jnp.concatenate([y_lat, y_ctx], axis=0), (1, 0, 2))


def _diff_kernel(with_lat, lam_scale, *refs):
    if with_lat:
        (q_ref, kc_ref, vc_ref, kl_ref, vl_ref, lam_ref, sg_ref, o_ref,
         q1_ref, q2_ref, m_ref, l_ref, acc_ref) = refs
    else:
        (q_ref, kc_ref, vc_ref, lam_ref, sg_ref, yin_ref, o_ref,
         q1_ref, q2_ref, m_ref, l_ref, acc_ref) = refs
    ki = pl.program_id(3)
    nk = pl.num_programs(3)

    def update(j, qm, kt, v):
        s = jnp.dot(qm, kt, preferred_element_type=F32)
        m_prev = m_ref[j]
        m_new = jnp.maximum(m_prev, jnp.max(s, axis=-1, keepdims=True))
        alpha = jnp.exp(m_prev - m_new)
        p = jnp.exp(s - m_new[:, 0:1])
        l_ref[j] = alpha * l_ref[j] + jnp.sum(p, axis=-1, keepdims=True)
        acc_ref[j] = alpha * acc_ref[j] + jnp.dot(p.astype(BF16), v, preferred_element_type=F32)
        m_ref[j] = m_new

    def both(kt, v):
        update(0, q1_ref[...], kt, v)
        update(1, q2_ref[...], kt, v)

    @pl.when(ki == 0)
    def _():
        q = q_ref[0]
        first = lax.broadcasted_iota(jnp.int32, q.shape, 1) < DIFF_DH
        q1_ref[...] = jnp.where(first, q, jnp.zeros_like(q))
        q2_ref[...] = jnp.where(first, jnp.zeros_like(q), q)
        m_ref[...] = jnp.full_like(m_ref, -jnp.inf)
        l_ref[...] = jnp.zeros_like(l_ref)
        acc_ref[...] = jnp.zeros_like(acc_ref)
        both(kc_ref[0], vc_ref[0])

    if with_lat:
        both(kl_ref[0], vl_ref[0])

    @pl.when(ki == nk - 1)
    def _():
        o = acc_ref[0] / l_ref[0] - lam_ref[...] * (acc_ref[1] / l_ref[1])
        y = o * lax.rsqrt(jnp.mean(o * o, axis=-1, keepdims=True) + RMS_EPS)
        o_ref[0] = ((y * sg_ref[...]) * lam_scale).astype(o_ref.dtype)


def _diff_attn(qk, kT, a, lam, subln_g, lam_init, L, y_prev, interpret):
    B, S, _ = qk.shape
    C = S - L
    hw = 2 * DIFF_DH
    lam_v = jnp.broadcast_to(lam.astype(F32).reshape(1, 1), (1, hw))
    sg = subln_g.astype(F32).reshape(1, hw)
    with_lat = y_prev is None
    tq = ATT_TQ if with_lat else C
    tk = min(ATT_TK, L)
    cb = L // C
    kc_spec = pl.BlockSpec((1, hw, C), lambda b, h, qi, ki: (b, h, cb))
    vc_spec = pl.BlockSpec((1, C, hw), lambda b, h, qi, ki: (b, cb, h))
    vec = pl.BlockSpec((1, hw), lambda b, h, qi, ki: (0, 0))
    scratch = [pltpu.VMEM((tq, hw), BF16), pltpu.VMEM((tq, hw), BF16), pltpu.VMEM((2, tq, hw), F32),
               pltpu.VMEM((2, tq, hw), F32), pltpu.VMEM((2, tq, hw), F32)]
    if with_lat:
        grid = (B, DIFF_H, L // tq, L // tk)
        in_specs = [pl.BlockSpec((1, tq, hw), lambda b, h, qi, ki: (b, qi, h)), kc_spec, vc_spec,
                    pl.BlockSpec((1, hw, tk), lambda b, h, qi, ki: (b, h, ki)),
                    pl.BlockSpec((1, tk, hw), lambda b, h, qi, ki: (b, ki, h)), vec, vec]
        args = (qk, kT, a, kT, a, lam_v, sg)
        out_spec = pl.BlockSpec((1, tq, hw), lambda b, h, qi, ki: (b, qi, h))
        aliases = {}
    else:
        grid = (B, DIFF_H, 1, 1)
        in_specs = [pl.BlockSpec((1, C, hw), lambda b, h, qi, ki: (b, cb, h)), kc_spec, vc_spec, vec, vec,
                    pl.BlockSpec(memory_space=pl.ANY)]
        args = (qk, kT, a, lam_v, sg, y_prev)
        out_spec = pl.BlockSpec((1, C, hw), lambda b, h, qi, ki: (b, cb, h))
        aliases = {5: 0}
    return pl.pallas_call(
        functools.partial(_diff_kernel, with_lat, 1.0 - lam_init),
        grid=grid, in_specs=in_specs, out_specs=out_spec,
        out_shape=jax.ShapeDtypeStruct((B, S, DIFF_W), BF16),
        scratch_shapes=scratch, input_output_aliases=aliases,
        compiler_params=_cparams(("parallel", "parallel", "parallel", "arbitrary")),
        interpret=interpret, name="diff_attn_lat" if with_lat else "diff_attn_ctx",
    )(*args)


def _na_bias_tables(rpb, rows):
    W = GRID_W
    ndr, ndc = 2 * NA_WIN_H - 1, 2 * NA_WIN_W - 1
    hi = lax.Precision.HIGHEST
    cols = np.arange(W)
    cstart = np.clip(cols - NA_WIN_W // 2, 0, W - NA_WIN_W)
    col_ok = (cols[None, :] >= cstart[:, None]) & (cols[None, :] < cstart[:, None] + NA_WIN_W)
    dc = np.clip(cols[None, :] - cols[:, None] + NA_WIN_W - 1, 0, ndc - 1)
    pick_dc = (dc[None] == np.arange(ndc)[:, None, None]).astype(np.float32)
    by_col = jnp.einsum('hrc,cqk->hrqk', rpb.astype(F32), pick_dc, precision=hi)
    a = np.arange(NA_ROWS)
    j = np.arange(NA_UNION)
    tabs = []
    for r0 in (0, NA_ROWS * 2, rows - NA_ROWS):
        u0 = int(np.clip(r0 - NA_WIN_H // 2, 0, rows - NA_UNION))
        r = r0 + a
        kr = u0 + j
        start = np.clip(r - NA_WIN_H // 2, 0, rows - NA_WIN_H)
        row_ok = (kr[None, :] >= start[:, None]) & (kr[None, :] < start[:, None] + NA_WIN_H)
        dr = np.clip(kr[None, :] - r[:, None] + NA_WIN_H - 1, 0, ndr - 1)
        pick_dr = (dr[..., None] == np.arange(ndr)).astype(np.float32)
        t = jnp.einsum('ajr,hrqk->haqjk', pick_dr, by_col, precision=hi)
        ok = row_ok[:, None, :, None] & col_ok[None, :, None, :]
        t = jnp.where(ok[None], t, NEG_INF)
        tabs.append(t.reshape(t.shape[0], NA_ROWS * W, NA_UNION * W))
    return jnp.stack(tabs)


def _na_kernel(rows, q_ref, k_ref, v_ref, bias_ref, o_ref):
    i = pl.program_id(1)
    L = rows * GRID_W
    u0 = jnp.clip(i * NA_ROWS - NA_WIN_H // 2, 0, rows - NA_UNION)
    ks = pl.multiple_of(u0 * GRID_W, GRID_W)
    nloc = NA_UNION * GRID_W
    q = q_ref[0]
    kw = k_ref[0, pl.ds(ks, nloc), :]
    vw = v_ref[0, pl.ds(ks, nloc), :]
    kc = k_ref[0, L:, :]
    vc = v_ref[0, L:, :]
    dn = (((1,), (1,)), ((), ()))
    outs = []
    for h in range(NA_H):
        sl = slice(h * NA_DH, (h + 1) * NA_DH)
        qh = q[:, sl]
        b = bias_ref[0, h]
        s_loc = lax.dot_general(qh, kw[:, sl], dn, preferred_element_type=F32)
        s_loc = jnp.where(b > 0.5 * NEG_INF, s_loc + b, NEG_INF)
        s_ctx = lax.dot_general(qh, kc[:, sl], dn, preferred_element_type=F32)
        m = jnp.maximum(jnp.max(s_loc, axis=-1, keepdims=True), jnp.max(s_ctx, axis=-1, keepdims=True))
        p_loc = jnp.exp(s_loc - m)
        p_ctx = jnp.exp(s_ctx - m)
        den = jnp.sum(p_loc, axis=-1, keepdims=True) + jnp.sum(p_ctx, axis=-1, keepdims=True)
        o = (jnp.dot(p_loc.astype(BF16), vw[:, sl], preferred_element_type=F32)
             + jnp.dot(p_ctx.astype(BF16), vc[:, sl], preferred_element_type=F32))
        outs.append(o / den)
    o_ref[0] = jnp.concatenate(outs, axis=-1).astype(o_ref.dtype)


def _na_ctx_kernel(q_ref, k_ref, v_ref, yin_ref, o_ref):
    q = q_ref[0]
    k = k_ref[0]
    v = v_ref[0]
    dn = (((1,), (1,)), ((), ()))
    outs = []
    for h in range(NA_H):
        sl = slice(h * NA_DH, (h + 1) * NA_DH)
        s = lax.dot_general(q[:, sl], k[:, sl], dn, preferred_element_type=F32)
        p = jnp.exp(s - jnp.max(s, axis=-1, keepdims=True))
        o = jnp.dot(p.astype(BF16), v[:, sl], preferred_element_type=F32)
        outs.append(o / jnp.sum(p, axis=-1, keepdims=True))
    o_ref[0] = jnp.concatenate(outs, axis=-1).astype(o_ref.dtype)


def _na_attn(a, rpb, L, with_ctx, interpret):
    B, S, _ = a.shape
    C = S - L
    rows = L // GRID_W
    assert rows >= NA_UNION and rows % NA_ROWS == 0
    nb = rows // NA_ROWS
    qn, kn = NA_ROWS * GRID_W, NA_UNION * GRID_W
    bias = _na_bias_tables(rpb, rows)
    qcol, kcol, vcol = DIFF_W // NA_W, DIFF_W // NA_W + 1, DIFF_W // NA_W + 2
    y = pl.pallas_call(
        functools.partial(_na_kernel, rows),
        grid=(B, nb),
        in_specs=[pl.BlockSpec((1, qn, NA_W), lambda b, i: (b, i, qcol)),
                  pl.BlockSpec((1, S, NA_W), lambda b, i: (b, 0, kcol)),
                  pl.BlockSpec((1, S, NA_W), lambda b, i: (b, 0, vcol)),
                  pl.BlockSpec((1, NA_H, qn, kn),
                               lambda b, i: (jnp.where(i == 0, 0, jnp.where(i == nb - 1, 2, 1)), 0, 0, 0))],
        out_specs=pl.BlockSpec((1, qn, NA_W), lambda b, i: (b, i, 0)),
        out_shape=jax.ShapeDtypeStruct((B, S, NA_W), BF16),
        compiler_params=_cparams(("parallel", "arbitrary")),
        interpret=interpret, name="na_attn_lat",
    )(a, a, a, bias)
    if not with_ctx:
        return y
    cb = L // C
    return pl.pallas_call(
        _na_ctx_kernel,
        grid=(B,),
        in_specs=[pl.BlockSpec((1, C, NA_W), lambda b: (b, cb, qcol)),
                  pl.BlockSpec((1, C, NA_W), lambda b: (b, cb, kcol)),
                  pl.BlockSpec((1, C, NA_W), lambda b: (b, cb, vcol)),
                  pl.BlockSpec(memory_space=pl.ANY)],
        out_specs=pl.BlockSpec((1, C, NA_W), lambda b: (b, cb, 0)),
        out_shape=jax.ShapeDtypeStruct((B, S, NA_W), BF16),
        input_output_aliases={3: 0},
        compiler_params=_cparams(("parallel",)),
        interpret=interpret, name="na_attn_ctx",
    )(a, a, a, y)


def _merge_kernel(x_ref, g_ref, ys_ref, yd_ref, yn_ref, mod_ref, ng_ref, wglu_ref, wbs_ref, wbd_ref,
                  wbn_ref, wo_ref, wq_ref, xo_ref, h2t_ref, qp_ref):
    ys = _gelu(ys_ref[0]).astype(BF16)
    z = jnp.dot(ys, wglu_ref[...], preferred_element_type=F32)
    ssm = (z[:, :SSM_W] * _sigmoid(z[:, SSM_W:])).astype(BF16)
    g = g_ref[0]
    m = (_sigmoid(g[:, 0:D]) * jnp.dot(ssm, wbs_ref[...], preferred_element_type=F32)
         + _sigmoid(g[:, D:2 * D]) * jnp.dot(yd_ref[0], wbd_ref[...], preferred_element_type=F32)
         + _sigmoid(g[:, 2 * D:3 * D]) * jnp.dot(yn_ref[0], wbn_ref[...], preferred_element_type=F32))
    mix = jnp.dot(m.astype(BF16), wo_ref[...], preferred_element_type=F32)
    x = x_ref[0] + mod_ref[0, 0, 2:3, :] * mix
    xo_ref[0] = x
    h2 = _rms_mod(x, ng_ref[...], mod_ref[0, 0, 4:5, :], mod_ref[0, 0, 3:4, :])
    h2t_ref[...] = h2.T.astype(BF16)
    q = jnp.dot(h2.astype(BF16), wq_ref[...], preferred_element_type=F32)
    for j in range(2 * PEER_H):
        qp_ref[j] = q[:, j * PEER_NK:(j + 1) * PEER_NK]


def _merge(x, g, ys, yd, yn, mods, ng2, wglu, wbs, wbd, wbn, wo, wq, L, sq, interpret):
    B, S, _ = x.shape
    T = TOK_BLK
    nl = L // T
    nt = sq // T
    tok = lambda b, i: (b, i, 0)
    ws = [wglu, wbs, wbd, wbn, wo, wq]
    return pl.pallas_call(
        _merge_kernel,
        grid=(B, nt),
        in_specs=[pl.BlockSpec((1, T, D), tok), pl.BlockSpec((1, T, 3 * D), tok),
                  pl.BlockSpec((1, T, SSM_W), tok), pl.BlockSpec((1, T, DIFF_W), tok),
                  pl.BlockSpec((1, T, NA_W), tok),
                  pl.BlockSpec((1, 1, 6, D), lambda b, i: (b, jnp.where(i >= nl, 1, 0), 0, 0)),
                  _const_spec((1, D))] + [_const_spec(w.shape) for w in ws],
        out_specs=[pl.BlockSpec((1, T, D), tok),
                   pl.BlockSpec((D, T), lambda b, i: (0, b * nt + i)),
                   pl.BlockSpec((2 * PEER_H, T, PEER_NK), lambda b, i: (0, b * nt + i, 0))],
        out_shape=[jax.ShapeDtypeStruct((B, sq, D), F32), jax.ShapeDtypeStruct((D, B * sq), BF16),
                   jax.ShapeDtypeStruct((2 * PEER_H, B * sq, PEER_NK), F32)],
        compiler_params=_cparams(("parallel", "parallel")),
        interpret=interpret, name="merge_branches",
    )(x, g, ys, yd, yn, mods, ng2.reshape(1, D), *ws)


def _route_kernel(q_ref, sk_ref, e0_ref, n0_ref, r1_ref, e1_ref, v0_ref, v1_ref):
    dn = (((1,), (1,)), ((), ()))
    ninf = jnp.float32(-jnp.inf)

    def head(h, _):
        s0 = lax.dot_general(sk_ref[2 * h], q_ref[2 * h].astype(BF16), dn, preferred_element_type=F32)
        s1 = lax.dot_general(sk_ref[2 * h + 1], q_ref[2 * h + 1].astype(BF16), dn, preferred_element_type=F32)
        s = s1
        r1 = jnp.full(s1.shape, float(PEER_NK), F32)
        for q in range(PEER_TOPK):
            m = jnp.max(s, axis=0, keepdims=True)
            v1_ref[q:q + 1, :] = m
            hit = s == m
            r1 = jnp.where(hit, float(q), r1)
            s = jnp.where(hit, ninf, s)
        s = s0
        for r in range(PEER_TOPK):
            m = jnp.max(s, axis=0, keepdims=True)
            v0_ref[r:r + 1, :] = m
            s = jnp.where(s == m, ninf, s)
        v1a = v1_ref[...]
        v1h = v1_ref[0:8, :]
        parts = [v0_ref[0:1, :] + v1a]
        for r in range(1, 8):
            parts.append(v0_ref[r:r + 1, :] + v1h)
        parts.append(v0_ref[8:16, :] + v1_ref[0:1, :])
        c = jnp.concatenate(parts, axis=0)
        cmax = v0_ref[0:1, :] + v1_ref[0:1, :]
        z = jnp.zeros_like(cmax)
        tau = cmax
        for _k in range(PEER_TOPK):
            tau = jnp.max(c, axis=0, keepdims=True)
            z = z + jnp.exp(tau - cmax)
            c = jnp.where(c == tau, ninf, c)
        n0 = jnp.zeros(s0.shape, F32)
        for q in range(PEER_TOPK):
            n0 = n0 + jnp.where(s0 + v1_ref[q:q + 1, :] >= tau, 1.0, 0.0)
        e0_ref[h] = jnp.exp(s0 - v0_ref[0:1, :]) * (1.0 / z)
        n0_ref[h] = n0
        r1_ref[h] = r1.astype(BF16)
        e1_ref[h] = jnp.exp(s1 - v1_ref[0:1, :]).astype(BF16)
        return 0

    lax.fori_loop(0, PEER_H, head, 0)


def _route(qp, sk, interpret):
    ntok = qp.shape[1]
    T = RT_BLK
    sspec = pl.BlockSpec((PEER_H, PEER_NK, T), lambda t: (0, 0, t))
    shp = (PEER_H, PEER_NK, ntok)
    return pl.pallas_call(
        _route_kernel,
        grid=(ntok // T,),
        in_specs=[pl.BlockSpec((2 * PEER_H, T, PEER_NK), lambda t: (0, t, 0)), _const_spec(sk.shape)],
        out_specs=[sspec] * 4,
        out_shape=[jax.ShapeDtypeStruct(shp, F32), jax.ShapeDtypeStruct(shp, F32),
                   jax.ShapeDtypeStruct(shp, BF16), jax.ShapeDtypeStruct(shp, BF16)],
        scratch_shapes=[pltpu.VMEM((PEER_TOPK, T), F32), pltpu.VMEM((PEER_TOPK, T), F32)],
        compiler_params=_cparams(("parallel",)),
        interpret=interpret, name="peer_route",
    )(qp, sk)


def _peer_kernel(h_ref, u_ref, vt_ref, e0_ref, n0_ref, r1_ref, e1_ref, o_ref,
                 act_a, act_b, wt_a, wt_b, acc_ref):
    e = pl.program_id(1)
    ne = PEER_E // PE_EB
    ni = PE_EB // PEER_NK

    @pl.when(e == 0)
    def _():
        acc_ref[...] = jnp.zeros_like(acc_ref)
        act_b[...] = jnp.zeros_like(act_b)
        wt_a[...] = jnp.zeros_like(wt_a)

    def stages(act_w, act_r, wt_w, wt_r):
        eb = jnp.clip(e - 1, 0, ne - 1)

        def gate_chunk(ii, rc, lt):
            i = eb * ni + ii
            rs = slice(rc * PE_RC, (rc + 1) * PE_RC)
            es = slice(ii * PEER_NK + rc * PE_RC, ii * PEER_NK + (rc + 1) * PE_RC)
            ls = slice(lt * PE_LC, (lt + 1) * PE_LC)
            g = jnp.zeros((PE_RC, PE_LC), BF16)
            for h in range(PEER_H):
                nb = jnp.broadcast_to(n0_ref[h, pl.ds(i, 1), ls], (16, PE_LC)).astype(BF16)
                eb0 = jnp.broadcast_to(e0_ref[h, pl.ds(i, 1), ls], (16, PE_LC)).astype(BF16)
                nb = jnp.concatenate([nb] * (PE_RC // 16), axis=0)
                eb0 = jnp.concatenate([eb0] * (PE_RC // 16), axis=0)
                w = jnp.where(r1_ref[h, rs, ls] < nb, e1_ref[h, rs, ls], jnp.zeros((), BF16))
                g = g + w * eb0
            wt_w[es, ls] = g * _gelu(act_r[es, ls]).astype(BF16)

        chunks = [(ii, rc, lt) for ii in range(ni) for rc in range(PEER_NK // PE_RC)
                  for lt in range(PE_T // PE_LC)]
        pieces = [(mh, nt) for mh in range(2) for nt in range(PE_T // PE_LC)]
        per = len(chunks) // len(pieces)
        for p, (mh, nt) in enumerate(pieces):
            ms = slice(mh * (PE_EB // 2), (mh + 1) * (PE_EB // 2))
            ds_ = slice(mh * (D // 2), (mh + 1) * (D // 2))
            ns = slice(nt * PE_LC, (nt + 1) * PE_LC)
            act_w[ms, ns] = jnp.dot(u_ref[ms, :], h_ref[:, ns], preferred_element_type=F32)
            for ck in chunks[p * per:(p + 1) * per]:
                gate_chunk(*ck)
            acc_ref[ds_, ns] += jnp.dot(vt_ref[ds_, :], wt_r[:, ns], preferred_element_type=F32)

    @pl.when(e % 2 == 0)
    def _():
        stages(act_a, act_b, wt_b, wt_a)

    @pl.when(e % 2 == 1)
    def _():
        stages(act_b, act_a, wt_a, wt_b)

    @pl.when(e == ne + 1)
    def _():
        o_ref[...] = acc_ref[...].T


def _peer(h2t, u_b, vt_b, e0, n0, r1, e1, interpret):
    ntok = h2t.shape[1]
    T, EB = PE_T, PE_EB
    ne = PEER_E // EB
    sspec = pl.BlockSpec((PEER_H, PEER_NK, T), lambda t, e: (0, 0, t))
    return pl.pallas_call(
        _peer_kernel,
        grid=(ntok // T, ne + 2),
        in_specs=[pl.BlockSpec((D, T), lambda t, e: (0, t)),
                  pl.BlockSpec((EB, D), lambda t, e: (jnp.minimum(e, ne - 1), 0)),
                  pl.BlockSpec((D, EB), lambda t, e: (0, jnp.clip(e - 2, 0, ne - 1))),
                  sspec, sspec, sspec, sspec],
        out_specs=pl.BlockSpec((T, D), lambda t, e: (t, 0)),
        out_shape=jax.ShapeDtypeStruct((ntok, D), F32),
        scratch_shapes=[pltpu.VMEM((EB, T), F32), pltpu.VMEM((EB, T), F32),
                        pltpu.VMEM((EB, T), BF16), pltpu.VMEM((EB, T), BF16), pltpu.VMEM((D, T), F32)],
        compiler_params=_cparams(("parallel", "arbitrary")),
        interpret=interpret, name="peer_dense",
    )(h2t, u_b, vt_b, e0, n0, r1, e1)


def _final_kernel(x_ref, pe_ref, g2_ref, ng_ref, o_ref):
    x = x_ref[0] + g2_ref[0] * pe_ref[0]
    y = x * lax.rsqrt(jnp.mean(x * x, axis=-1, keepdims=True) + RMS_EPS)
    o_ref[0] = y * ng_ref[...]


def _final(x, peer, g2, ng, interpret):
    B, L, _ = x.shape
    T = TOK_BLK
    tok = lambda b, i: (b, i, 0)
    return pl.pallas_call(
        _final_kernel,
        grid=(B, L // T),
        in_specs=[pl.BlockSpec((1, T, D), tok), pl.BlockSpec((1, T, D), tok),
                  pl.BlockSpec((1, 1, D), lambda b, i: (b, 0, 0)), _const_spec((1, D))],
        out_specs=pl.BlockSpec((1, T, D), tok),
        out_shape=jax.ShapeDtypeStruct((B, L, D), F32),
        compiler_params=_cparams(("parallel", "parallel")),
        interpret=interpret, name="final_norm",
    )(x, peer, g2, ng.reshape(1, D))


def _rope_tables(L, C):
    t = jnp.arange(L)
    freqs = ROPE_THETA ** (-jnp.arange(ROPE_F, dtype=F32) / ROPE_F)
    ang_r = (t // GRID_W).astype(F32)[:, None] * freqs
    ang_c = (t % GRID_W).astype(F32)[:, None] * freqs
    cos64 = jnp.concatenate([jnp.cos(ang_r)] * 2 + [jnp.cos(ang_c)] * 2, axis=-1)
    sin64 = jnp.concatenate([-jnp.sin(ang_r), jnp.sin(ang_r), -jnp.sin(ang_c), jnp.sin(ang_c)], axis=-1)
    reps = DIFF_W // DIFF_DH
    scale = DIFF_DH ** -0.5
    cos_l = jnp.concatenate([jnp.tile(cos64, (1, reps)) * scale, jnp.tile(cos64, (1, reps))], axis=-1)
    sin_l = jnp.concatenate([jnp.tile(sin64, (1, reps)) * scale, jnp.tile(sin64, (1, reps))], axis=-1)
    cos_c = jnp.concatenate([jnp.full((C, DIFF_W), scale, F32), jnp.ones((C, DIFF_W), F32)], axis=-1)
    return (jnp.concatenate([cos_l, cos_c], axis=0),
            jnp.concatenate([sin_l, jnp.zeros((C, 2 * DIFF_W), F32)], axis=0))


def _split_w_in(w):
    o = np.cumsum([0, SSM_W, DIFF_W, DIFF_W, DIFF_W, NA_W, NA_W, NA_W, D, D, D])
    seg = lambda k: w[:, o[k]:o[k + 1]]
    wg = jnp.concatenate([seg(7), seg(8), seg(9)], axis=1)
    wqk = jnp.concatenate([seg(1), seg(2)], axis=1)
    j = np.arange(2 * DIFF_W)
    partner = np.where(j % 32 < ROPE_F, j + ROPE_F, j - ROPE_F)
    wqs = wqk[:, partner]
    wr = jnp.concatenate([seg(3), seg(4) * (NA_DH ** -0.5), seg(5), seg(6), seg(0)], axis=1)
    return wg.astype(BF16), wqk.astype(BF16), wqs.astype(BF16), wr.astype(BF16)


def _forward(x, c, ctx, c_ctx, ada_w, ada_b, norm1_g, norm2_g, w_in, ssm_lambda_re, ssm_lambda_im,
             ssm_log_step, ssm_b_re, ssm_b_im, ssm_c_re, ssm_c_im, ssm_d, ssm_glu_w, diff_lambda,
             diff_subln_g, na_rpb, w_br_ssm, w_br_diff, w_br_na, w_out, peer_wq, peer_subkeys, peer_u,
             peer_v, final_norm_g, interpret=False):
    B, L, _ = x.shape
    C = ctx.shape[1]
    S = L + C
    depth = ada_w.shape[0]
    assert C == TOK_BLK and L % ATT_TQ == 0 and L % C == 0
    xs = jnp.concatenate([x, ctx], axis=1).astype(F32)
    c8 = jnp.concatenate([c, c_ctx[None], jnp.zeros((8 - B - 1, D), c.dtype)], axis=0).astype(F32)
    cos_t, sin_t = _rope_tables(L, C)

    def layer_mods(l):
        m = _ada_mod(c8, ada_w[l].astype(F32), ada_b[l].astype(F32), interpret).reshape(8, 6, D)
        lat = m[:B]
        return jnp.stack([lat, jnp.broadcast_to(m[B][None], lat.shape)], axis=1)

    peer_out = None
    prev_mods = None
    for l in range(depth):
        with_ctx = l < depth - 1
        lam_init = 0.8 - 0.6 * math.exp(-0.3 * l)
        mods = layer_mods(l)
        both = jnp.concatenate([prev_mods if prev_mods is not None else jnp.zeros_like(mods), mods], axis=2)
        wg, wqk, wqs, wr = _split_w_in(w_in[l])
        xs, g, qk, a, u = _proj_in(xs, peer_out, norm1_g[l].astype(F32), both, cos_t, sin_t,
                                   wg, wqk, wqs, wr, L, interpret)
        ys = _ssm(u, L, ssm_lambda_re[l], ssm_lambda_im[l], ssm_log_step[l], ssm_b_re[l], ssm_b_im[l],
                  ssm_c_re[l], ssm_c_im[l], ssm_d[l], interpret)
        lq = diff_lambda[l].astype(F32)
        lam = jnp.exp(jnp.sum(lq[0] * lq[1])) - jnp.exp(jnp.sum(lq[2] * lq[3])) + lam_init
        kT = jnp.transpose(qk[:, :, DIFF_W:], (0, 2, 1))
        yd = _diff_attn(qk, kT, a, lam, diff_subln_g[l], lam_init, L, None, interpret)
        if with_ctx:
            yd = _diff_attn(qk, kT, a, lam, diff_subln_g[l], lam_init, L, yd, interpret)
        yn = _na_attn(a, na_rpb[l], L, with_ctx, interpret)
        sq = S if with_ctx else L
        xs2, h2t, qp = _merge(xs, g, ys, yd, yn, mods, norm2_g[l].astype(F32),
                              ssm_glu_w[l].astype(BF16), w_br_ssm[l].astype(BF16),
                              w_br_diff[l].astype(BF16), w_br_na[l].astype(BF16), w_out[l].astype(BF16),
                              peer_wq[l].astype(BF16), L, sq, interpret)
        sk = peer_subkeys[l].astype(BF16).reshape(2 * PEER_H, PEER_NK, -1)
        e0, n0, r1, e1 = _route(qp, sk, interpret)
        pe = _peer(h2t, peer_u[l].astype(BF16), jnp.transpose(peer_v[l].astype(BF16)), e0, n0, r1, e1,
                   interpret)
        peer_out = pe.reshape(B, sq, D)
        xs = xs2
        prev_mods = mods
    g2 = prev_mods[:, 0, 5:6, :]
    return _final(xs, peer_out, g2, final_norm_g.astype(F32), interpret)


def kernel(x, c, ctx, c_ctx, ada_w, ada_b, norm1_g, norm2_g, w_in, ssm_lambda_re, ssm_lambda_im, ssm_log_step, ssm_b_re, ssm_b_im, ssm_c_re, ssm_c_im, ssm_d, ssm_glu_w, diff_lambda, diff_subln_g, na_rpb, w_br_ssm, w_br_diff, w_br_na, w_out, peer_wq, peer_subkeys, peer_u, peer_v, final_norm_g):
    return _forward(x, c, ctx, c_ctx, ada_w, ada_b, norm1_g, norm2_g, w_in, ssm_lambda_re, ssm_lambda_im,
                    ssm_log_step, ssm_b_re, ssm_b_im, ssm_c_re, ssm_c_im, ssm_d, ssm_glu_w, diff_lambda,
                    diff_subln_g, na_rpb, w_br_ssm, w_br_diff, w_br_na, w_out, peer_wq, peer_subkeys,
                    peer_u, peer_v, final_norm_g)
```

```python
import functools
import math

import numpy as np
import jax
import jax.numpy as jnp
from jax import lax
from jax.experimental import pallas as pl
from jax.experimental.pallas import tpu as pltpu

D = 1024
GRID_W = 64
SSM_W = 256
SSM_G = 16
SSM_P = 16
SSM_N = 64
SSM_RE_MAX = -1e-4
DIFF_H = 4
DIFF_DH = 64
DIFF_W = 512
NA_H = 4
NA_DH = 64
NA_W = 256
NA_WIN_H = 8
NA_WIN_W = 16
ROPE_THETA = 10000.0
ROPE_F = 16
PEER_H = 8
PEER_NK = 128
PEER_E = PEER_NK * PEER_NK
PEER_TOPK = 16
RMS_EPS = 1e-6
NEG_INF = -1e30

VMEM_LIMIT = 56 * 1024 * 1024
TOK_BLK = 256
SSM_TT = 128
ATT_TQ = 1024
ATT_TK = 1024
NA_ROWS = 4
NA_UNION = NA_ROWS + NA_WIN_H
RT_BLK = 256
PE_T = 512
PE_EB = 1024
PE_RC = 64
PE_LC = 256

F32 = jnp.float32
BF16 = jnp.bfloat16


def _cparams(sem, flags=None):
    return pltpu.CompilerParams(dimension_semantics=sem, vmem_limit_bytes=VMEM_LIMIT, flags=flags)


def _const_spec(shape):
    nd = len(shape)
    return pl.BlockSpec(shape, lambda *_: (0,) * nd)


def _rms_mod(x, ng, sc, sh):
    y = x * lax.rsqrt(jnp.mean(x * x, axis=-1, keepdims=True) + RMS_EPS)
    return (y * ng) * (1.0 + sc) + sh


def _gelu(x):
    return 0.5 * x * (1.0 + jnp.tanh(0.7978845608028654 * (x + 0.044715 * (x * x * x))))


def _sigmoid(x):
    return 1.0 / (1.0 + jnp.exp(-x))


def _ada_kernel(c_ref, w_ref, b_ref, o_ref):
    c = c_ref[...]
    s = c * _sigmoid(c)
    o_ref[...] = jnp.dot(s, w_ref[...], preferred_element_type=F32,
                         precision=lax.Precision.HIGHEST) + b_ref[...]


def _ada_mod(c8, w, b, interpret):
    n = w.shape[1]
    tn = 1536
    return pl.pallas_call(
        _ada_kernel,
        grid=(n // tn,),
        in_specs=[pl.BlockSpec((8, D), lambda j: (0, 0)),
                  pl.BlockSpec((D, tn), lambda j: (0, j)),
                  pl.BlockSpec((1, tn), lambda j: (0, j))],
        out_specs=pl.BlockSpec((8, tn), lambda j: (0, j)),
        out_shape=jax.ShapeDtypeStruct((8, n), F32),
        compiler_params=_cparams(("arbitrary",)),
        interpret=interpret, name="ada_mod",
    )(c8, w, b.reshape(1, n))


def _proj_kernel(has_prev, *refs):
    if has_prev:
        (x_ref, pe_ref, ng_ref, mod_ref, cos_ref, sin_ref, wg_ref, wqk_ref, wqs_ref, wr_ref,
         xo_ref, g_ref, qk_ref, a_ref, u_ref) = refs
        x = x_ref[0] + mod_ref[0, 0, 5:6, :] * pe_ref[0]
        xo_ref[0] = x
    else:
        (x_ref, ng_ref, mod_ref, cos_ref, sin_ref, wg_ref, wqk_ref, wqs_ref, wr_ref,
         g_ref, qk_ref, a_ref, u_ref) = refs
        x = x_ref[0]
    mod_ref = mod_ref
    h = _rms_mod(x, ng_ref[...], mod_ref[0, 0, 7:8, :], mod_ref[0, 0, 6:7, :]).astype(BF16)
    g_ref[0] = jnp.dot(h, wg_ref[...], preferred_element_type=F32)
    qk = jnp.dot(h, wqk_ref[...], preferred_element_type=F32)
    qs = jnp.dot(h, wqs_ref[...], preferred_element_type=F32)
    qk_ref[0] = (qk * cos_ref[...] + qs * sin_ref[...]).astype(BF16)
    r = jnp.dot(h, wr_ref[...], preferred_element_type=F32)
    na = DIFF_W + 3 * NA_W
    a_ref[0] = r[:, :na].astype(BF16)
    u_ref[0] = r[:, na:]


def _proj_in(x, peer, ng, mods, cos_t, sin_t, wg, wqk, wqs, wr, L, interpret):
    B, S, _ = x.shape
    T = TOK_BLK
    nl = L // T
    has_prev = peer is not None
    tok = lambda b, i: (b, i, 0)
    tab = lambda b, i: (i, 0)
    in_specs = [pl.BlockSpec((1, T, D), tok)]
    args = [x]
    if has_prev:
        in_specs.append(pl.BlockSpec((1, T, D), tok))
        args.append(peer)
    in_specs += [_const_spec((1, D)),
                 pl.BlockSpec((1, 1, 12, D), lambda b, i: (b, jnp.where(i >= nl, 1, 0), 0, 0)),
                 pl.BlockSpec((T, 2 * DIFF_W), tab), pl.BlockSpec((T, 2 * DIFF_W), tab),
                 _const_spec(wg.shape), _const_spec(wqk.shape), _const_spec(wqs.shape),
                 _const_spec(wr.shape)]
    args += [ng.reshape(1, D), mods, cos_t, sin_t, wg, wqk, wqs, wr]
    na = DIFF_W + 3 * NA_W
    out_specs = [pl.BlockSpec((1, T, 3 * D), tok), pl.BlockSpec((1, T, 2 * DIFF_W), tok),
                 pl.BlockSpec((1, T, na), tok), pl.BlockSpec((1, T, SSM_W), tok)]
    out_shape = [jax.ShapeDtypeStruct((B, S, 3 * D), F32), jax.ShapeDtypeStruct((B, S, 2 * DIFF_W), BF16),
                 jax.ShapeDtypeStruct((B, S, na), BF16), jax.ShapeDtypeStruct((B, S, SSM_W), F32)]
    if has_prev:
        out_specs = [pl.BlockSpec((1, T, D), tok)] + out_specs
        out_shape = [jax.ShapeDtypeStruct((B, S, D), F32)] + out_shape
    outs = pl.pallas_call(
        functools.partial(_proj_kernel, has_prev),
        grid=(B, S // T), in_specs=in_specs, out_specs=out_specs, out_shape=out_shape,
        compiler_params=_cparams(("parallel", "parallel")),
        interpret=interpret, name="proj_in",
    )(*args)
    if has_prev:
        return outs
    return [x] + list(outs)


def _ssm_kernel(uf_ref, ub_ref, bm_ref, are_ref, aim_ref, cm_ref, d_ref, yf_ref, yb_ref,
                lhs_ref, bu_ref, y_ref, st_ref):
    tt = uf_ref.shape[0]
    nb = uf_ref.shape[1]
    hs = SSM_G * SSM_N

    @pl.when(pl.program_id(0) == 0)
    def _():
        st_ref[...] = jnp.zeros_like(st_ref)
        lhs_ref[...] = jnp.zeros_like(lhs_ref)

    lhs_ref[:, 0:nb, 0:SSM_W] = uf_ref[...]
    lhs_ref[:, nb:2 * nb, SSM_W:2 * SSM_W] = ub_ref[...]
    lhs = lhs_ref[...].reshape(tt * 8, 2 * SSM_W).astype(BF16)
    bu_ref[...] = jnp.dot(lhs, bm_ref[...], preferred_element_type=F32)

    def step(t, carry):
        sre, sim = carry
        r = pl.multiple_of(t * 8, 8)
        are = are_ref[...]
        aim = aim_ref[...]
        nre = are * sre - aim * sim + bu_ref[pl.ds(r, 8), 0:hs]
        nim = are * sim + aim * sre + bu_ref[pl.ds(r, 8), hs:2 * hs]
        bu_ref[pl.ds(r, 8), 0:hs] = nre
        bu_ref[pl.ds(r, 8), hs:2 * hs] = nim
        return nre, nim

    sre, sim = lax.fori_loop(0, tt, step, (st_ref[:, 0:hs], st_ref[:, hs:2 * hs]))
    st_ref[:, 0:hs] = sre
    st_ref[:, hs:2 * hs] = sim

    y = jnp.dot(bu_ref[...].astype(BF16), cm_ref[...], preferred_element_type=F32)
    y_ref[...] = y.reshape(tt, 8, 2 * SSM_W)
    yf_ref[...] = y_ref[:, 0:nb, 0:SSM_W] + d_ref[...] * uf_ref[...]
    yb_ref[...] = y_ref[:, nb:2 * nb, SSM_W:2 * SSM_W]


def _ssm_params(lam_re, lam_im, log_step, b_re, b_im, c_re, c_im, nb):
    lam = lax.complex(jnp.minimum(lam_re.astype(F32), SSM_RE_MAX), lam_im.astype(F32))
    lam_dt = lam * jnp.exp(log_step.astype(F32))[..., None]
    lam_bar = jnp.exp(lam_dt)
    b = lax.complex(b_re.astype(F32), b_im.astype(F32))
    b_bar = ((lam_bar - 1.0) / lam)[..., None] * b
    eye = jnp.eye(SSM_G, dtype=F32)
    hs = SSM_G * SSM_N

    def bmat(part):
        return jnp.einsum('gh,dhnp->dgphn', eye, part).reshape(2 * SSM_W, hs)

    bm = jnp.concatenate([bmat(jnp.real(b_bar)), bmat(jnp.imag(b_bar))], axis=1)

    def cmat(part):
        return jnp.einsum('hg,dgpn->hndgp', eye, part).reshape(hs, 2 * SSM_W)

    cm = jnp.concatenate([cmat(c_re.astype(F32)), -cmat(c_im.astype(F32))], axis=0)
    a_re = jnp.repeat(jnp.real(lam_bar).reshape(2, hs), nb, axis=0)
    a_im = jnp.repeat(jnp.imag(lam_bar).reshape(2, hs), nb, axis=0)
    return bm.astype(BF16), a_re, a_im, cm.astype(BF16)


def _ssm(u, L, lam_re, lam_im, log_step, b_re, b_im, c_re, c_im, d, interpret):
    B, S, _ = u.shape
    assert 2 * B == 8, "the scan packs (direction, batch) onto the 8 sublanes"
    bm, a_re, a_im, cm = _ssm_params(lam_re, lam_im, log_step, b_re, b_im, c_re, c_im, B)
    ul, uc = u[:, :L], u[:, L:]
    uf = jnp.transpose(jnp.concatenate([uc, ul], axis=1), (1, 0, 2))
    ub = jnp.transpose(jnp.concatenate([uc[:, ::-1], ul[:, ::-1]], axis=1), (1, 0, 2))
    tt = SSM_TT
    hs2 = 2 * SSM_G * SSM_N
    blk = pl.BlockSpec((tt, B, SSM_W), lambda i: (i, 0, 0))
    yf, yb = pl.pallas_call(
        _ssm_kernel,
        grid=(S // tt,),
        in_specs=[blk, blk, _const_spec(bm.shape), _const_spec(a_re.shape), _const_spec(a_im.shape),
                  _const_spec(cm.shape), _const_spec((1, 1, SSM_W))],
        out_specs=[blk, blk],
        out_shape=[jax.ShapeDtypeStruct((S, B, SSM_W), F32)] * 2,
        scratch_shapes=[pltpu.VMEM((tt, 8, 2 * SSM_W), F32), pltpu.VMEM((tt * 8, hs2), F32),
                        pltpu.VMEM((tt, 8, 2 * SSM_W), F32), pltpu.VMEM((8, hs2), F32)],
        compiler_params=_cparams(("arbitrary",)),
        interpret=interpret, name="ssm_scan",
    )(uf, ub, bm, a_re, a_im, cm, d.astype(F32).reshape(1, 1, SSM_W))
    C = S - L
    y_ctx = yf[:C] + yb[:C][::-1]
    y_lat = yf[C:] + yb[C:][::-1]
    return jnp.transpose(jnp.concatenate([y_lat, y_ctx], axis=0), (1, 0, 2))


def _diff_kernel(with_lat, lam_scale, *refs):
    if with_lat:
        (q_ref, kc_ref, vc_ref, kl_ref, vl_ref, lam_ref, sg_ref, o_ref,
         q1_ref, q2_ref, m_ref, acc_ref, al_ref, s_ref, p_ref) = refs
    else:
        (q_ref, kc_ref, vc_ref, lam_ref, sg_ref, o_ref,
         q1_ref, q2_ref, m_ref, acc_ref, al_ref, s_ref, p_ref) = refs
    ki = pl.program_id(3)
    nk = pl.num_programs(3)
    tq = q_ref.shape[1]
    hw = 2 * DIFF_DH
    rg = 16

    def update(j, qm_ref, kt, v):
        n = kt.shape[1]
        s_ref[:, 0:n] = jnp.dot(qm_ref[...], kt, preferred_element_type=F32)

        for r in range(tq // rg):
            rs = slice(r * rg, (r + 1) * rg)
            tiles = [s_ref[rs, c * hw:(c + 1) * hw] for c in range(n // hw)]
            pm = tiles[0]
            for t in tiles[1:]:
                pm = jnp.maximum(pm, t)
            m_prev = m_ref[j, rs, :]
            m_new = jnp.maximum(m_prev, jnp.max(pm, axis=-1, keepdims=True))
            for c, t in enumerate(tiles):
                p_ref[rs, c * hw:(c + 1) * hw] = jnp.exp((t - m_new).astype(BF16))
            al_ref[rs, :] = jnp.exp(m_prev - m_new)
            m_ref[j, rs, :] = m_new
        va = jnp.concatenate([v, jnp.ones_like(v)], axis=-1)
        pv = jnp.dot(p_ref[:, 0:n], va, preferred_element_type=F32)
        alpha = al_ref[...]
        acc_ref[j, :, 0:hw] = alpha * acc_ref[j, :, 0:hw] + pv[:, 0:hw]
        acc_ref[j, :, hw:2 * hw] = alpha * acc_ref[j, :, hw:2 * hw] + pv[:, hw:2 * hw]

    def both(kt, v):
        update(0, q1_ref, kt, v)
        update(1, q2_ref, kt, v)

    @pl.when(ki == 0)
    def _():
        q = q_ref[0]
        first = lax.broadcasted_iota(jnp.int32, q.shape, 1) < DIFF_DH
        q1_ref[...] = jnp.where(first, q, jnp.zeros_like(q))
        q2_ref[...] = jnp.where(first, jnp.zeros_like(q), q)
        m_ref[...] = jnp.full_like(m_ref, -jnp.inf)
        acc_ref[...] = jnp.zeros_like(acc_ref)
        both(kc_ref[0], vc_ref[0])

    if with_lat:
        both(kl_ref[0], vl_ref[0])

    @pl.when(ki == nk - 1)
    def _():
        o = (acc_ref[0, :, 0:hw] / acc_ref[0, :, hw:2 * hw]
             - lam_ref[...] * (acc_ref[1, :, 0:hw] / acc_ref[1, :, hw:2 * hw]))
        y = o * lax.rsqrt(jnp.mean(o * o, axis=-1, keepdims=True) + RMS_EPS)
        o_ref[0] = ((y * sg_ref[...]) * lam_scale).astype(o_ref.dtype)


def _diff_attn(qk, kT, a, lam, subln_g, lam_init, L, with_lat, interpret):
    B, S, _ = qk.shape
    C = S - L
    hw = 2 * DIFF_DH
    lam_v = jnp.broadcast_to(lam.astype(F32).reshape(1, 1), (1, hw))
    sg = subln_g.astype(F32).reshape(1, hw)
    tq = min(ATT_TQ, L) if with_lat else C
    tk = min(ATT_TK, L) if with_lat else C
    cb = L // C
    kc_spec = pl.BlockSpec((1, hw, C), lambda b, h, qi, ki: (b, h, cb))
    vc_spec = pl.BlockSpec((1, C, hw), lambda b, h, qi, ki: (b, cb, h))
    vec = pl.BlockSpec((1, hw), lambda b, h, qi, ki: (0, 0))
    scratch = [pltpu.VMEM((tq, hw), BF16), pltpu.VMEM((tq, hw), BF16), pltpu.VMEM((2, tq, hw), F32),
               pltpu.VMEM((2, tq, 2 * hw), F32), pltpu.VMEM((tq, hw), F32),
               pltpu.VMEM((tq, tk), F32), pltpu.VMEM((tq, tk), BF16)]
    if with_lat:
        grid = (B, DIFF_H, L // tq, L // tk)
        in_specs = [pl.BlockSpec((1, tq, hw), lambda b, h, qi, ki: (b, qi, h)), kc_spec, vc_spec,
                    pl.BlockSpec((1, hw, tk), lambda b, h, qi, ki: (b, h, ki)),
                    pl.BlockSpec((1, tk, hw), lambda b, h, qi, ki: (b, ki, h)), vec, vec]
        args = (qk, kT, a, kT, a, lam_v, sg)
        out_spec = pl.BlockSpec((1, tq, hw), lambda b, h, qi, ki: (b, qi, h))
        rows_out = L
    else:
        grid = (B, DIFF_H, 1, 1)
        in_specs = [pl.BlockSpec((1, C, hw), lambda b, h, qi, ki: (b, cb, h)), kc_spec, vc_spec, vec, vec]
        args = (qk, kT, a, lam_v, sg)
        out_spec = pl.BlockSpec((1, C, hw), lambda b, h, qi, ki: (b, 0, h))
        rows_out = C
    return pl.pallas_call(
        functools.partial(_diff_kernel, with_lat, 1.0 - lam_init),
        grid=grid, in_specs=in_specs, out_specs=out_spec,
        out_shape=jax.ShapeDtypeStruct((B, rows_out, DIFF_W), BF16),
        scratch_shapes=scratch,
        compiler_params=_cparams(("parallel", "parallel", "parallel", "arbitrary")),
        interpret=interpret, name="diff_attn_lat" if with_lat else "diff_attn_ctx",
    )(*args)


def _na_bias_tables(rpb, rows):
    W = GRID_W
    ndr, ndc = 2 * NA_WIN_H - 1, 2 * NA_WIN_W - 1
    hi = lax.Precision.HIGHEST
    cols = np.arange(W)
    cstart = np.clip(cols - NA_WIN_W // 2, 0, W - NA_WIN_W)
    col_ok = (cols[None, :] >= cstart[:, None]) & (cols[None, :] < cstart[:, None] + NA_WIN_W)
    dc = np.clip(cols[None, :] - cols[:, None] + NA_WIN_W - 1, 0, ndc - 1)
    pick_dc = (dc[None] == np.arange(ndc)[:, None, None]).astype(np.float32)
    by_col = jnp.einsum('hrc,cqk->hrqk', rpb.astype(F32), pick_dc, precision=hi)
    a = np.arange(NA_ROWS)
    j = np.arange(NA_UNION)
    tabs = []
    for r0 in (0, NA_ROWS * 2, rows - NA_ROWS):
        u0 = int(np.clip(r0 - NA_WIN_H // 2, 0, rows - NA_UNION))
        r = r0 + a
        kr = u0 + j
        start = np.clip(r - NA_WIN_H // 2, 0, rows - NA_WIN_H)
        row_ok = (kr[None, :] >= start[:, None]) & (kr[None, :] < start[:, None] + NA_WIN_H)
        dr = np.clip(kr[None, :] - r[:, None] + NA_WIN_H - 1, 0, ndr - 1)
        pick_dr = (dr[..., None] == np.arange(ndr)).astype(np.float32)
        t = jnp.einsum('ajr,hrqk->haqjk', pick_dr, by_col, precision=hi)
        ok = row_ok[:, None, :, None] & col_ok[None, :, None, :]
        t = jnp.where(ok[None], t, NEG_INF)
        tabs.append(t.reshape(t.shape[0], NA_ROWS * W, NA_UNION * W))
    return jnp.stack(tabs)


def _na_kernel(rows, q_ref, k_ref, v_ref, bias_ref, o_ref):
    i = pl.program_id(1)
    L = rows * GRID_W
    u0 = jnp.clip(i * NA_ROWS - NA_WIN_H // 2, 0, rows - NA_UNION)
    ks = pl.multiple_of(u0 * GRID_W, GRID_W)
    nloc = NA_UNION * GRID_W
    q = q_ref[0]
    kw = k_ref[0, pl.ds(ks, nloc), :]
    vw = v_ref[0, pl.ds(ks, nloc), :]
    kc = k_ref[0, L:, :]
    vc = v_ref[0, L:, :]
    dn = (((1,), (1,)), ((), ()))
    outs = []
    for h in range(NA_H):
        sl = slice(h * NA_DH, (h + 1) * NA_DH)
        qh = q[:, sl]
        b = bias_ref[0, h]
        s_loc = lax.dot_general(qh, kw[:, sl], dn, preferred_element_type=F32)
        s_loc = jnp.where(b > 0.5 * NEG_INF, s_loc + b, NEG_INF)
        s_ctx = lax.dot_general(qh, kc[:, sl], dn, preferred_element_type=F32)
        m = jnp.maximum(jnp.max(s_loc, axis=-1, keepdims=True), jnp.max(s_ctx, axis=-1, keepdims=True))
        p_loc = jnp.exp(s_loc - m)
        p_ctx = jnp.exp(s_ctx - m)
        den = jnp.sum(p_loc, axis=-1, keepdims=True) + jnp.sum(p_ctx, axis=-1, keepdims=True)
        o = (jnp.dot(p_loc.astype(BF16), vw[:, sl], preferred_element_type=F32)
             + jnp.dot(p_ctx.astype(BF16), vc[:, sl], preferred_element_type=F32))
        outs.append(o / den)
    o_ref[0] = jnp.concatenate(outs, axis=-1).astype(o_ref.dtype)


def _na_ctx_kernel(q_ref, k_ref, v_ref, o_ref):
    q = q_ref[0]
    k = k_ref[0]
    v = v_ref[0]
    dn = (((1,), (1,)), ((), ()))
    outs = []
    for h in range(NA_H):
        sl = slice(h * NA_DH, (h + 1) * NA_DH)
        s = lax.dot_general(q[:, sl], k[:, sl], dn, preferred_element_type=F32)
        p = jnp.exp(s - jnp.max(s, axis=-1, keepdims=True))
        o = jnp.dot(p.astype(BF16), v[:, sl], preferred_element_type=F32)
        outs.append(o / jnp.sum(p, axis=-1, keepdims=True))
    o_ref[0] = jnp.concatenate(outs, axis=-1).astype(o_ref.dtype)


def _na_attn(a, rpb, L, with_ctx, interpret):
    B, S, _ = a.shape
    C = S - L
    rows = L // GRID_W
    assert rows >= NA_UNION and rows % NA_ROWS == 0
    nb = rows // NA_ROWS
    qn, kn = NA_ROWS * GRID_W, NA_UNION * GRID_W
    bias = _na_bias_tables(rpb, rows)
    qcol, kcol, vcol = DIFF_W // NA_W, DIFF_W // NA_W + 1, DIFF_W // NA_W + 2
    y = pl.pallas_call(
        functools.partial(_na_kernel, rows),
        grid=(B, nb),
        in_specs=[pl.BlockSpec((1, qn, NA_W), lambda b, i: (b, i, qcol)),
                  pl.BlockSpec((1, S, NA_W), lambda b, i: (b, 0, kcol)),
                  pl.BlockSpec((1, S, NA_W), lambda b, i: (b, 0, vcol)),
                  pl.BlockSpec((1, NA_H, qn, kn),
                               lambda b, i: (jnp.where(i == 0, 0, jnp.where(i == nb - 1, 2, 1)), 0, 0, 0))],
        out_specs=pl.BlockSpec((1, qn, NA_W), lambda b, i: (b, i, 0)),
        out_shape=jax.ShapeDtypeStruct((B, L, NA_W), BF16),
        compiler_params=_cparams(("parallel", "arbitrary")),
        interpret=interpret, name="na_attn_lat",
    )(a, a, a, bias)
    if not with_ctx:
        return y
    cb = L // C
    yc = pl.pallas_call(
        _na_ctx_kernel,
        grid=(B,),
        in_specs=[pl.BlockSpec((1, C, NA_W), lambda b: (b, cb, qcol)),
                  pl.BlockSpec((1, C, NA_W), lambda b: (b, cb, kcol)),
                  pl.BlockSpec((1, C, NA_W), lambda b: (b, cb, vcol))],
        out_specs=pl.BlockSpec((1, C, NA_W), lambda b: (b, 0, 0)),
        out_shape=jax.ShapeDtypeStruct((B, C, NA_W), BF16),
        compiler_params=_cparams(("parallel",)),
        interpret=interpret, name="na_attn_ctx",
    )(a, a, a)
    return jnp.concatenate([y, yc], axis=1)


def _merge_kernel(x_ref, g_ref, ys_ref, yd_ref, yn_ref, mod_ref, ng_ref, wglu_ref, wbs_ref, wbd_ref,
                  wbn_ref, wo_ref, wq_ref, xo_ref, h2t_ref, qp_ref):
    ys = _gelu(ys_ref[0]).astype(BF16)
    z = jnp.dot(ys, wglu_ref[...], preferred_element_type=F32)
    ssm = (z[:, :SSM_W] * _sigmoid(z[:, SSM_W:])).astype(BF16)
    g = g_ref[0]
    m = (_sigmoid(g[:, 0:D]) * jnp.dot(ssm, wbs_ref[...], preferred_element_type=F32)
         + _sigmoid(g[:, D:2 * D]) * jnp.dot(yd_ref[0], wbd_ref[...], preferred_element_type=F32)
         + _sigmoid(g[:, 2 * D:3 * D]) * jnp.dot(yn_ref[0], wbn_ref[...], preferred_element_type=F32))
    mix = jnp.dot(m.astype(BF16), wo_ref[...], preferred_element_type=F32)
    x = x_ref[0] + mod_ref[0, 0, 2:3, :] * mix
    xo_ref[0] = x
    h2 = _rms_mod(x, ng_ref[...], mod_ref[0, 0, 4:5, :], mod_ref[0, 0, 3:4, :])
    h2t_ref[...] = h2.T.astype(BF16)
    q = jnp.dot(h2.astype(BF16), wq_ref[...], preferred_element_type=F32)
    for j in range(2 * PEER_H):
        qp_ref[j] = q[:, j * PEER_NK:(j + 1) * PEER_NK]


def _merge(x, g, ys, yd, yn, mods, ng2, wglu, wbs, wbd, wbn, wo, wq, L, sq, interpret):
    B, S, _ = x.shape
    T = TOK_BLK
    nl = L // T
    nt = sq // T
    tok = lambda b, i: (b, i, 0)
    ws = [wglu, wbs, wbd, wbn, wo, wq]
    return pl.pallas_call(
        _merge_kernel,
        grid=(B, nt),
        in_specs=[pl.BlockSpec((1, T, D), tok), pl.BlockSpec((1, T, 3 * D), tok),
                  pl.BlockSpec((1, T, SSM_W), tok), pl.BlockSpec((1, T, DIFF_W), tok),
                  pl.BlockSpec((1, T, NA_W), tok),
                  pl.BlockSpec((1, 1, 6, D), lambda b, i: (b, jnp.where(i >= nl, 1, 0), 0, 0)),
                  _const_spec((1, D))] + [_const_spec(w.shape) for w in ws],
        out_specs=[pl.BlockSpec((1, T, D), tok),
                   pl.BlockSpec((D, T), lambda b, i: (0, b * nt + i)),
                   pl.BlockSpec((2 * PEER_H, T, PEER_NK), lambda b, i: (0, b * nt + i, 0))],
        out_shape=[jax.ShapeDtypeStruct((B, sq, D), F32), jax.ShapeDtypeStruct((D, B * sq), BF16),
                   jax.ShapeDtypeStruct((2 * PEER_H, B * sq, PEER_NK), F32)],
        compiler_params=_cparams(("parallel", "parallel")),
        interpret=interpret, name="merge_branches",
    )(x, g, ys, yd, yn, mods, ng2.reshape(1, D), *ws)


def _route_kernel(q_ref, sk_ref, e0_ref, n0_ref, r1_ref, e1_ref, v0_ref, v1_ref):
    dn = (((1,), (1,)), ((), ()))
    ninf = jnp.float32(-jnp.inf)

    def head(h, _):
        s0 = lax.dot_general(sk_ref[2 * h], q_ref[2 * h].astype(BF16), dn, preferred_element_type=F32)
        s1 = lax.dot_general(sk_ref[2 * h + 1], q_ref[2 * h + 1].astype(BF16), dn, preferred_element_type=F32)
        s = s1
        r1 = jnp.full(s1.shape, float(PEER_NK), F32)
        for q in range(PEER_TOPK):
            m = jnp.max(s, axis=0, keepdims=True)
            v1_ref[q:q + 1, :] = m
            hit = s == m
            r1 = jnp.where(hit, float(q), r1)
            s = jnp.where(hit, ninf, s)
        s = s0
        for r in range(PEER_TOPK):
            m = jnp.max(s, axis=0, keepdims=True)
            v0_ref[r:r + 1, :] = m
            s = jnp.where(s == m, ninf, s)
        v1a = v1_ref[...]
        v1h = v1_ref[0:8, :]
        parts = [v0_ref[0:1, :] + v1a]
        for r in range(1, 8):
            parts.append(v0_ref[r:r + 1, :] + v1h)
        parts.append(v0_ref[8:16, :] + v1_ref[0:1, :])
        c = jnp.concatenate(parts, axis=0)
        cmax = v0_ref[0:1, :] + v1_ref[0:1, :]
        z = jnp.zeros_like(cmax)
        tau = cmax
        for _k in range(PEER_TOPK):
            tau = jnp.max(c, axis=0, keepdims=True)
            z = z + jnp.exp(tau - cmax)
            c = jnp.where(c == tau, ninf, c)
        n0 = jnp.zeros(s0.shape, F32)
        for q in range(PEER_TOPK):
            n0 = n0 + jnp.where(s0 + v1_ref[q:q + 1, :] >= tau, 1.0, 0.0)
        e0_ref[h] = jnp.exp(s0 - v0_ref[0:1, :]) * (1.0 / z)
        n0_ref[h] = n0
        r1_ref[h] = r1.astype(BF16)
        e1_ref[h] = jnp.exp(s1 - v1_ref[0:1, :]).astype(BF16)
        return 0

    lax.fori_loop(0, PEER_H, head, 0)


def _route(qp, sk, interpret):
    ntok = qp.shape[1]
    T = RT_BLK
    sspec = pl.BlockSpec((PEER_H, PEER_NK, T), lambda t: (0, 0, t))
    shp = (PEER_H, PEER_NK, ntok)
    return pl.pallas_call(
        _route_kernel,
        grid=(ntok // T,),
        in_specs=[pl.BlockSpec((2 * PEER_H, T, PEER_NK), lambda t: (0, t, 0)), _const_spec(sk.shape)],
        out_specs=[sspec] * 4,
        out_shape=[jax.ShapeDtypeStruct(shp, F32), jax.ShapeDtypeStruct(shp, F32),
                   jax.ShapeDtypeStruct(shp, BF16), jax.ShapeDtypeStruct(shp, BF16)],
        scratch_shapes=[pltpu.VMEM((PEER_TOPK, T), F32), pltpu.VMEM((PEER_TOPK, T), F32)],
        compiler_params=_cparams(("parallel",)),
        interpret=interpret, name="peer_route",
    )(qp, sk)


def _peer_kernel(h_ref, u_ref, vt_ref, e0_ref, n0_ref, r1_ref, e1_ref, o_ref,
                 act_a, act_b, wt_a, wt_b, acc_ref):
    e = pl.program_id(1)
    ne = PEER_E // PE_EB
    ni = PE_EB // PEER_NK

    @pl.when(e == 0)
    def _():
        acc_ref[...] = jnp.zeros_like(acc_ref)
        act_b[...] = jnp.zeros_like(act_b)
        wt_a[...] = jnp.zeros_like(wt_a)

    def stages(act_w, act_r, wt_w, wt_r):
        eb = jnp.clip(e - 1, 0, ne - 1)

        def gate_chunk(ii, rc, lt):
            i = eb * ni + ii
            rs = slice(rc * PE_RC, (rc + 1) * PE_RC)
            es = slice(ii * PEER_NK + rc * PE_RC, ii * PEER_NK + (rc + 1) * PE_RC)
            ls = slice(lt * PE_LC, (lt + 1) * PE_LC)
            g = jnp.zeros((PE_RC, PE_LC), BF16)
            for h in range(PEER_H):
                nb = jnp.broadcast_to(n0_ref[h, pl.ds(i, 1), ls], (16, PE_LC)).astype(BF16)
                eb0 = jnp.broadcast_to(e0_ref[h, pl.ds(i, 1), ls], (16, PE_LC)).astype(BF16)
                nb = jnp.concatenate([nb] * (PE_RC // 16), axis=0)
                eb0 = jnp.concatenate([eb0] * (PE_RC // 16), axis=0)
                w = jnp.where(r1_ref[h, rs, ls] < nb, e1_ref[h, rs, ls], jnp.zeros((), BF16))
                g = g + w * eb0
            wt_w[es, ls] = g * _gelu(act_r[es, ls]).astype(BF16)

        chunks = [(ii, rc, lt) for ii in range(ni) for rc in range(PEER_NK // PE_RC)
                  for lt in range(PE_T // PE_LC)]
        pieces = [(mh, nt) for mh in range(2) for nt in range(PE_T // PE_LC)]
        per = len(chunks) // len(pieces)
        for p, (mh, nt) in enumerate(pieces):
            ms = slice(mh * (PE_EB // 2), (mh + 1) * (PE_EB // 2))
            ds_ = slice(mh * (D // 2), (mh + 1) * (D // 2))
            ns = slice(nt * PE_LC, (nt + 1) * PE_LC)
            act_w[ms, ns] = jnp.dot(u_ref[ms, :], h_ref[:, ns], preferred_element_type=F32)
            for ck in chunks[p * per:(p + 1) * per]:
                gate_chunk(*ck)
            acc_ref[ds_, ns] += jnp.dot(vt_ref[ds_, :], wt_r[:, ns], preferred_element_type=F32)

    @pl.when(e % 2 == 0)
    def _():
        stages(act_a, act_b, wt_b, wt_a)

    @pl.when(e % 2 == 1)
    def _():
        stages(act_b, act_a, wt_a, wt_b)

    @pl.when(e == ne + 1)
    def _():
        o_ref[...] = acc_ref[...].T


def _peer(h2t, u_b, vt_b, e0, n0, r1, e1, interpret):
    ntok = h2t.shape[1]
    T, EB = PE_T, PE_EB
    ne = PEER_E // EB
    sspec = pl.BlockSpec((PEER_H, PEER_NK, T), lambda t, e: (0, 0, t))
    return pl.pallas_call(
        _peer_kernel,
        grid=(ntok // T, ne + 2),
        in_specs=[pl.BlockSpec((D, T), lambda t, e: (0, t)),
                  pl.BlockSpec((EB, D), lambda t, e: (jnp.minimum(e, ne - 1), 0)),
                  pl.BlockSpec((D, EB), lambda t, e: (0, jnp.clip(e - 2, 0, ne - 1))),
                  sspec, sspec, sspec, sspec],
        out_specs=pl.BlockSpec((T, D), lambda t, e: (t, 0)),
        out_shape=jax.ShapeDtypeStruct((ntok, D), F32),
        scratch_shapes=[pltpu.VMEM((EB, T), F32), pltpu.VMEM((EB, T), F32),
                        pltpu.VMEM((EB, T), BF16), pltpu.VMEM((EB, T), BF16), pltpu.VMEM((D, T), F32)],
        compiler_params=_cparams(("parallel", "arbitrary")),
        interpret=interpret, name="peer_dense",
    )(h2t, u_b, vt_b, e0, n0, r1, e1)


def _final_kernel(x_ref, pe_ref, g2_ref, ng_ref, o_ref):
    x = x_ref[0] + g2_ref[0] * pe_ref[0]
    y = x * lax.rsqrt(jnp.mean(x * x, axis=-1, keepdims=True) + RMS_EPS)
    o_ref[0] = y * ng_ref[...]


def _final(x, peer, g2, ng, interpret):
    B, L, _ = x.shape
    T = TOK_BLK
    tok = lambda b, i: (b, i, 0)
    return pl.pallas_call(
        _final_kernel,
        grid=(B, L // T),
        in_specs=[pl.BlockSpec((1, T, D), tok), pl.BlockSpec((1, T, D), tok),
                  pl.BlockSpec((1, 1, D), lambda b, i: (b, 0, 0)), _const_spec((1, D))],
        out_specs=pl.BlockSpec((1, T, D), tok),
        out_shape=jax.ShapeDtypeStruct((B, L, D), F32),
        compiler_params=_cparams(("parallel", "parallel")),
        interpret=interpret, name="final_norm",
    )(x, peer, g2, ng.reshape(1, D))


def _rope_tables(L, C):
    t = jnp.arange(L)
    freqs = ROPE_THETA ** (-jnp.arange(ROPE_F, dtype=F32) / ROPE_F)
    ang_r = (t // GRID_W).astype(F32)[:, None] * freqs
    ang_c = (t % GRID_W).astype(F32)[:, None] * freqs
    cos64 = jnp.concatenate([jnp.cos(ang_r)] * 2 + [jnp.cos(ang_c)] * 2, axis=-1)
    sin64 = jnp.concatenate([-jnp.sin(ang_r), jnp.sin(ang_r), -jnp.sin(ang_c), jnp.sin(ang_c)], axis=-1)
    reps = DIFF_W // DIFF_DH
    scale = DIFF_DH ** -0.5
    cos_l = jnp.concatenate([jnp.tile(cos64, (1, reps)) * scale, jnp.tile(cos64, (1, reps))], axis=-1)
    sin_l = jnp.concatenate([jnp.tile(sin64, (1, reps)) * scale, jnp.tile(sin64, (1, reps))], axis=-1)
    cos_c = jnp.concatenate([jnp.full((C, DIFF_W), scale, F32), jnp.ones((C, DIFF_W), F32)], axis=-1)
    return (jnp.concatenate([cos_l, cos_c], axis=0),
            jnp.concatenate([sin_l, jnp.zeros((C, 2 * DIFF_W), F32)], axis=0))


def _split_w_in(w):
    o = np.cumsum([0, SSM_W, DIFF_W, DIFF_W, DIFF_W, NA_W, NA_W, NA_W, D, D, D])
    seg = lambda k: w[:, o[k]:o[k + 1]]
    wg = jnp.concatenate([seg(7), seg(8), seg(9)], axis=1)
    wqk = jnp.concatenate([seg(1), seg(2)], axis=1)
    j = np.arange(2 * DIFF_W)
    partner = np.where(j % 32 < ROPE_F, j + ROPE_F, j - ROPE_F)
    wqs = wqk[:, partner]
    wr = jnp.concatenate([seg(3), seg(4) * (NA_DH ** -0.5), seg(5), seg(6), seg(0)], axis=1)
    return wg.astype(BF16), wqk.astype(BF16), wqs.astype(BF16), wr.astype(BF16)


def _forward(x, c, ctx, c_ctx, ada_w, ada_b, norm1_g, norm2_g, w_in, ssm_lambda_re, ssm_lambda_im,
             ssm_log_step, ssm_b_re, ssm_b_im, ssm_c_re, ssm_c_im, ssm_d, ssm_glu_w, diff_lambda,
             diff_subln_g, na_rpb, w_br_ssm, w_br_diff, w_br_na, w_out, peer_wq, peer_subkeys, peer_u,
             peer_v, final_norm_g, interpret=False):
    B, L, _ = x.shape
    C = ctx.shape[1]
    S = L + C
    depth = ada_w.shape[0]
    assert C == TOK_BLK and L % ATT_TQ == 0 and L % C == 0
    xs = jnp.concatenate([x, ctx], axis=1).astype(F32)
    c8 = jnp.concatenate([c, c_ctx[None], jnp.zeros((8 - B - 1, D), c.dtype)], axis=0).astype(F32)
    cos_t, sin_t = _rope_tables(L, C)

    def layer_mods(l):
        m = _ada_mod(c8, ada_w[l].astype(F32), ada_b[l].astype(F32), interpret).reshape(8, 6, D)
        lat = m[:B]
        return jnp.stack([lat, jnp.broadcast_to(m[B][None], lat.shape)], axis=1)

    peer_out = None
    prev_mods = None
    for l in range(depth):
        with_ctx = l < depth - 1
        lam_init = 0.8 - 0.6 * math.exp(-0.3 * l)
        mods = layer_mods(l)
        both = jnp.concatenate([prev_mods if prev_mods is not None else jnp.zeros_like(mods), mods], axis=2)
        wg, wqk, wqs, wr = _split_w_in(w_in[l])
        xs, g, qk, a, u = _proj_in(xs, peer_out, norm1_g[l].astype(F32), both, cos_t, sin_t,
                                   wg, wqk, wqs, wr, L, interpret)
        ys = _ssm(u, L, ssm_lambda_re[l], ssm_lambda_im[l], ssm_log_step[l], ssm_b_re[l], ssm_b_im[l],
                  ssm_c_re[l], ssm_c_im[l], ssm_d[l], interpret)
        lq = diff_lambda[l].astype(F32)
        lam = jnp.exp(jnp.sum(lq[0] * lq[1])) - jnp.exp(jnp.sum(lq[2] * lq[3])) + lam_init
        kT = jnp.transpose(qk[:, :, DIFF_W:], (0, 2, 1))
        yd = _diff_attn(qk, kT, a, lam, diff_subln_g[l], lam_init, L, True, interpret)
        if with_ctx:
            yd = jnp.concatenate(
                [yd, _diff_attn(qk, kT, a, lam, diff_subln_g[l], lam_init, L, False, interpret)], axis=1)
        yn = _na_attn(a, na_rpb[l], L, with_ctx, interpret)
        sq = S if with_ctx else L
        xs2, h2t, qp = _merge(xs, g, ys, yd, yn, mods, norm2_g[l].astype(F32),
                              ssm_glu_w[l].astype(BF16), w_br_ssm[l].astype(BF16),
                              w_br_diff[l].astype(BF16), w_br_na[l].astype(BF16), w_out[l].astype(BF16),
                              peer_wq[l].astype(BF16), L, sq, interpret)
        sk = peer_subkeys[l].astype(BF16).reshape(2 * PEER_H, PEER_NK, -1)
        e0, n0, r1, e1 = _route(qp, sk, interpret)
        pe = _peer(h2t, peer_u[l].astype(BF16), jnp.transpose(peer_v[l].astype(BF16)), e0, n0, r1, e1,
                   interpret)
        peer_out = pe.reshape(B, sq, D)
        xs = xs2
        prev_mods = mods
    g2 = prev_mods[:, 0, 5:6, :]
    return _final(xs, peer_out, g2, final_norm_g.astype(F32), interpret)


def kernel(x, c, ctx, c_ctx, ada_w, ada_b, norm1_g, norm2_g, w_in, ssm_lambda_re, ssm_lambda_im, ssm_log_step, ssm_b_re, ssm_b_im, ssm_c_re, ssm_c_im, ssm_d, ssm_glu_w, diff_lambda, diff_subln_g, na_rpb, w_br_ssm, w_br_diff, w_br_na, w_out, peer_wq, peer_subkeys, peer_u, peer_v, final_norm_g):
    return _forward(x, c, ctx, c_ctx, ada_w, ada_b, norm1_g, norm2_g, w_in, ssm_lambda_re, ssm_lambda_im,
                    ssm_log_step, ssm_b_re, ssm_b_im, ssm_c_re, ssm_c_im, ssm_d, ssm_glu_w, diff_lambda,
                    diff_subln_g, na_rpb, w_br_ssm, w_br_diff, w_br_na, w_out, peer_wq, peer_subkeys,
                    peer_u, peer_v, final_norm_g)
```
